```python
import jax, jax.numpy as jnp
from jax import lax
import numpy as np

D_MODEL = 1024
BATCH = 8
SEQ = 2048
DEPTH = 4

N_META = 16
BLOCK = 128
WINDOW = 128
RET_HEADS = 8
RET_QK_DIM = D_MODEL // 16
RET_V_DIM = D_MODEL // 8
ATT_Q_HEADS = 8
ATT_KV_HEADS = 2
ATT_GROUP = ATT_Q_HEADS // ATT_KV_HEADS
ATT_HEAD_DIM = D_MODEL // 16
ROPE_DIM = ATT_HEAD_DIM // 4
ROPE_THETA = 500000.0
XPOS_THETA = 10000.0
D_FF = 4 * D_MODEL
EPS = 1e-6
NEG_INF = -1e30

RET_QK = RET_HEADS * RET_QK_DIM
RET_V = RET_HEADS * RET_V_DIM
ATT_Q = ATT_Q_HEADS * ATT_HEAD_DIM
ATT_KV = ATT_KV_HEADS * ATT_HEAD_DIM
SPLITS = (RET_QK, RET_QK, RET_V, RET_V, ATT_Q, ATT_KV, ATT_KV, D_MODEL, D_MODEL)
D_IN = sum(SPLITS)

kernel_name = "hybrid_retention_swa_gated_encoder"


def rms_norm(x, g):
    xf = x.astype(jnp.float32)
    y = xf * lax.rsqrt(jnp.mean(xf * xf, axis=-1, keepdims=True) + EPS)
    return (y * g.astype(jnp.float32)).astype(x.dtype)


def rotate(x, pos, theta, rot_dim):
    half = rot_dim // 2
    freqs = jnp.power(jnp.float32(theta), -jnp.arange(0, rot_dim, 2, dtype=jnp.float32) / rot_dim)
    ang = pos.astype(jnp.float32)[:, None] * freqs[None, :]
    cos = jnp.cos(ang)[:, None, :]
    sin = jnp.sin(ang)[:, None, :]
    xf = x.astype(jnp.float32)
    x1, x2 = xf[..., :half], xf[..., half:rot_dim]
    out = jnp.concatenate([x1 * cos - x2 * sin, x2 * cos + x1 * sin, xf[..., rot_dim:]], axis=-1)
    return out.astype(x.dtype)


def retention_dir(q, k, v, log_gamma, include_diag):
    C = q.shape[-2]
    idx = jnp.arange(C, dtype=jnp.float32)
    rel = idx[:, None] - idx[None, :]
    lg = log_gamma[:, None, None]
    mask = (rel >= 0) if include_diag else (rel > 0)
    decay_in = jnp.where(mask[None], jnp.exp(lg * jnp.maximum(rel, 0.0)[None]), 0.0)
    s = jnp.einsum('bhncd,bhnkd->bhnck', q, k) * decay_in[:, None]
    o_in = jnp.einsum('bhnck,bhnke->bhnce', s, v)
    k_dec = k * jnp.exp(lg * (C - 1 - idx)[None, :])[:, None, :, :].transpose(0, 1, 3, 2)
    kv = jnp.einsum('bhncd,bhnce->bhnde', k_dec, v)
    chunk_decay = jnp.exp(log_gamma * C)[:, None, None]

    def step(state, kv_n):
        return state * chunk_decay + kv_n, state

    init = jnp.zeros(kv.shape[:2] + kv.shape[3:], jnp.float32)
    _, s_prev = lax.scan(step, init, jnp.moveaxis(kv, 2, 0))
    s_prev = jnp.moveaxis(s_prev, 0, 2)
    q_dec = q * jnp.exp(lg * (idx + 1.0)[None, :])[:, None, :, :].transpose(0, 1, 3, 2)
    o_x = jnp.einsum('bhncd,bhnde->bhnce', q_dec, s_prev)
    return o_in + o_x


def retention_branch(q, k, v, gate, log_decay, pad):
    B, L = q.shape[:2]
    Lp = L + pad
    nc = Lp // BLOCK
    pos = jnp.arange(L)
    q = rotate(q, pos, XPOS_THETA, RET_QK_DIM) * (RET_QK_DIM ** -0.5)
    k = rotate(k, pos, XPOS_THETA, RET_QK_DIM)

    def chunk(t):
        t = jnp.pad(t.astype(jnp.float32), ((0, 0), (pad, 0), (0, 0), (0, 0)))
        return t.reshape(B, nc, BLOCK, t.shape[2], t.shape[3]).transpose(0, 3, 1, 2, 4)

    qc, kc, vc = chunk(q), chunk(k), chunk(v)
    flip = lambda t: t[:, :, ::-1, ::-1]
    o = retention_dir(qc, kc, vc, log_decay[0], True) + flip(
        retention_dir(flip(qc), flip(kc), flip(vc), log_decay[1], False))
    o = o.transpose(0, 2, 3, 1, 4).reshape(B, Lp, RET_HEADS, RET_V_DIM)[:, pad:]
    mu = jnp.mean(o, axis=-1, keepdims=True)
    var = jnp.mean(jnp.square(o - mu), axis=-1, keepdims=True)
    o = (o - mu) * lax.rsqrt(var + EPS)
    o = o.reshape(B, L, RET_V).astype(gate.dtype)
    return o * jax.nn.silu(gate)


def attention_branch(q, k, v, sink, pad):
    B, L = q.shape[:2]
    Lp = L + pad
    nb = Lp // BLOCK
    pos = jnp.arange(L)
    q = rotate(q, pos, ROPE_THETA, ROPE_DIM)
    k = rotate(k, pos, ROPE_THETA, ROPE_DIM)
    meta_k, meta_v = k[:, :N_META], v[:, :N_META]
    qb = jnp.pad(q, ((0, 0), (pad, 0), (0, 0), (0, 0))).reshape(
        B, nb, BLOCK, ATT_KV_HEADS, ATT_GROUP, ATT_HEAD_DIM)
    ext = ((0, 0), (pad + BLOCK, BLOCK), (0, 0), (0, 0))
    kb = jnp.pad(k, ext).reshape(B, nb + 2, BLOCK, ATT_KV_HEADS, ATT_HEAD_DIM)
    vb = jnp.pad(v, ext).reshape(B, nb + 2, BLOCK, ATT_KV_HEADS, ATT_HEAD_DIM)

    def band(t):
        return jnp.concatenate([t[:, :-2], t[:, 1:-1], t[:, 2:]], axis=2)

    kband, vband = band(kb), band(vb)
    qpos = jnp.arange(nb)[:, None] * BLOCK + jnp.arange(BLOCK)[None, :]
    kpos = (jnp.arange(nb)[:, None] - 1) * BLOCK + jnp.arange(3 * BLOCK)[None, :]
    valid = ((kpos[:, None, :] >= pad + N_META) & (kpos[:, None, :] < Lp)
             & (jnp.abs(qpos[:, :, None] - kpos[:, None, :]) <= WINDOW))
    scale = ATT_HEAD_DIM ** -0.5
    s_band = jnp.einsum('bnqgrd,bnkgd->bngrqk', qb, kband).astype(jnp.float32) * scale
    s_band = jnp.where(valid[None, :, None, None], s_band, NEG_INF)
    s_meta = jnp.einsum('bnqgrd,bmgd->bngrqm', qb, meta_k).astype(jnp.float32) * scale
    s_sink = jnp.broadcast_to(sink.astype(jnp.float32).reshape(1, 1, ATT_KV_HEADS, ATT_GROUP, 1, 1),
                              s_meta.shape[:-1] + (1,))
    p = jax.nn.softmax(jnp.concatenate([s_sink, s_meta, s_band], axis=-1), axis=-1).astype(v.dtype)
    o = (jnp.einsum('bngrqm,bmgd->bnqgrd', p[..., 1:1 + N_META], meta_v)
         + jnp.einsum('bngrqk,bnkgd->bnqgrd', p[..., 1 + N_META:], vband))
    return o.reshape(B, Lp, ATT_Q)[:, pad:]


def setup_inputs(seed: int = 0) -> dict:
    key = jax.random.key(seed)
    ks = jax.random.split(key, 14)
    f = jnp.float32
    nrm = lambda k, shape, s: jax.random.normal(k, shape, f) * s
    heads = np.arange(RET_HEADS)
    base = np.log(-np.log(1.0 - 2.0 ** (-5.0 - heads))).astype(np.float32)
    return {
        "x": nrm(ks[0], (BATCH, SEQ, D_MODEL), 1.0),
        "meta_tokens": nrm(ks[1], (N_META, D_MODEL), 1.0),
        "w_in": nrm(ks[2], (DEPTH, D_MODEL, D_IN), D_MODEL ** -0.5),
        "w_ret_o": nrm(ks[3], (DEPTH, RET_V, D_MODEL), RET_V ** -0.5),
        "w_att_o": nrm(ks[4], (DEPTH, ATT_Q, D_MODEL), ATT_Q ** -0.5),
        "w_mix_o": nrm(ks[5], (DEPTH, D_MODEL, D_MODEL), D_MODEL ** -0.5),
        "w_ff1": nrm(ks[6], (DEPTH, D_MODEL, D_FF), D_MODEL ** -0.5),
        "w_ff2": nrm(ks[7], (DEPTH, D_FF, D_MODEL), D_FF ** -0.5),
        "norm_mix_pre": 1.0 + nrm(ks[8], (DEPTH, D_MODEL), 0.02),
        "norm_mix_post": 1.0 + nrm(ks[9], (DEPTH, D_MODEL), 0.02),
        "norm_ff_pre": 1.0 + nrm(ks[10], (DEPTH, D_MODEL), 0.02),
        "norm_ff_post": 1.0 + nrm(ks[11], (DEPTH, D_MODEL), 0.02),
        "ret_decay": jnp.asarray(base)[None, None, :] + nrm(ks[12], (DEPTH, 2, RET_HEADS), 0.05),
        "attn_sink": nrm(ks[13], (DEPTH, ATT_Q_HEADS), 0.5),
    }


def reference(x, meta_tokens, w_in, w_ret_o, w_att_o, w_mix_o, w_ff1, w_ff2,
              norm_mix_pre, norm_mix_post, norm_ff_pre, norm_ff_post, ret_decay, attn_sink):
    B = x.shape[0]
    h = jnp.concatenate([jnp.broadcast_to(meta_tokens[None].astype(x.dtype), (B, N_META, D_MODEL)), x], axis=1)
    L = h.shape[1]
    pad = (-L) % BLOCK
    for l in range(DEPTH):
        u = rms_norm(h, norm_mix_pre[l])
        proj = u @ w_in[l]
        parts, off = [], 0
        for size in SPLITS:
            parts.append(proj[..., off:off + size])
            off += size
        q_r, k_r, v_r, g_r, q_a, k_a, v_a, gate_r, gate_a = parts
        log_decay = -jnp.exp(ret_decay[l].astype(jnp.float32))
        y_r = retention_branch(q_r.reshape(B, L, RET_HEADS, RET_QK_DIM),
                               k_r.reshape(B, L, RET_HEADS, RET_QK_DIM),
                               v_r.reshape(B, L, RET_HEADS, RET_V_DIM),
                               g_r, log_decay, pad) @ w_ret_o[l]
        y_a = attention_branch(q_a.reshape(B, L, ATT_Q_HEADS, ATT_HEAD_DIM),
                               k_a.reshape(B, L, ATT_KV_HEADS, ATT_HEAD_DIM),
                               v_a.reshape(B, L, ATT_KV_HEADS, ATT_HEAD_DIM),
                               attn_sink[l], pad) @ w_att_o[l]
        mix = (jax.nn.sigmoid(gate_r) * y_r + jax.nn.sigmoid(gate_a) * y_a) @ w_mix_o[l]
        h = h + rms_norm(mix, norm_mix_post[l])
        u = rms_norm(h, norm_ff_pre[l])
        ff = jnp.square(jax.nn.relu(u @ w_ff1[l])) @ w_ff2[l]
        h = h + rms_norm(ff, norm_ff_post[l])
    return h[:, N_META:]
```

```python
import functools

import jax
import jax.numpy as jnp
from jax import lax
from jax.experimental import pallas as pl
from jax.experimental.pallas import tpu as pltpu

F32 = jnp.float32
BF16 = jnp.bfloat16

D_MODEL = 1024
N_META = 16
CHUNK = 128
RET_HEADS = 8
RET_QK_DIM = 64
RET_V_DIM = 128
ATT_Q_HEADS = 8
ATT_KV_HEADS = 2
ATT_GROUP = ATT_Q_HEADS // ATT_KV_HEADS
ATT_HEAD_DIM = 64
ROPE_DIM = 16
ROPE_THETA = 500000.0
XPOS_THETA = 10000.0
D_FF = 4 * D_MODEL
EPS = 1e-6
NEG_INF = -1e30

RET_QK = RET_HEADS * RET_QK_DIM
RET_V = RET_HEADS * RET_V_DIM
ATT_Q = ATT_Q_HEADS * ATT_HEAD_DIM
ATT_KV = ATT_KV_HEADS * ATT_HEAD_DIM
OFF_QR = 0
OFF_KR = OFF_QR + RET_QK
OFF_VR = OFF_KR + RET_QK
OFF_GR = OFF_VR + RET_V
OFF_QA = OFF_GR + RET_V
OFF_KA = OFF_QA + ATT_Q
OFF_VA = OFF_KA + ATT_KV
MIX_COLS = OFF_VA + ATT_KV
GATE_COLS = 2 * D_MODEL
D_IN = MIX_COLS + GATE_COLS

LANES = 128
MXU_N = 256
VMEM_LIMIT = 56 * 1024 * 1024


def _dot(a, b):
    return jnp.dot(a, b, preferred_element_type=F32)


def _dot_nt(a, b):
    return lax.dot_general(a, b, (((1,), (1,)), ((), ())), preferred_element_type=F32)


def _rms(x, g):
    return x * lax.rsqrt(jnp.mean(x * x, axis=-1, keepdims=True) + EPS) * g


def _sigmoid(x):
    return 1.0 / (1.0 + jnp.exp(-x))


def _resident(shape):
    return pl.BlockSpec(shape, lambda *_: (0,) * len(shape), pipeline_mode=pl.Buffered(1))


def _rotary_tables(seq):
    pos = jnp.concatenate([jnp.arange(N_META, N_META + seq), jnp.arange(N_META)]).astype(F32)
    d = jnp.arange(LANES) % RET_QK_DIM

    def table(theta, rot_dim):
        half = rot_dim // 2
        freqs = jnp.power(jnp.float32(theta), -jnp.arange(0, rot_dim, 2, dtype=F32) / rot_dim)
        ang = pos[:, None] * freqs[None, :]
        cos, sin = jnp.cos(ang), jnp.sin(ang)
        idx = d % half
        rotated = (d < rot_dim)[None, :]
        cos_l = jnp.where(rotated, cos[:, idx], 1.0)
        sin_l = jnp.where(rotated, jnp.where((d < half)[None, :], -sin[:, idx], sin[:, idx]), 0.0)
        return cos_l, sin_l

    cr, sr = table(XPOS_THETA, RET_QK_DIM)
    ca, sa = table(ROPE_THETA, ROPE_DIM)
    qs_r = RET_QK_DIM ** -0.5
    qs_a = ATT_HEAD_DIM ** -0.5
    return (cr * qs_r, sr * qs_r, cr, sr, ca * qs_a, sa * qs_a, ca, sa)


def _rotate(x, cos, sin, half):
    lane = lax.broadcasted_iota(jnp.int32, x.shape, 1) % RET_QK_DIM
    partner = jnp.where(lane < half, pltpu.roll(x, LANES - half, 1), pltpu.roll(x, half, 1))
    return x * cos + partner * sin


IN_TM = 688


def _in_proj_kernel(h_ref, g_ref, w_ref, crq, srq, crk, srk, caq, saq, cak, sak, mix_ref, gate_ref):
    x = h_ref[...]
    u = _rms(x, g_ref[...]).astype(BF16)

    def proj(c0, n):
        return _dot(u, w_ref[:, c0:c0 + n])

    def rot_store(c0, cos_ref, sin_ref, half):
        p = proj(c0, MXU_N)
        for j in range(MXU_N // LANES):
            t = _rotate(p[:, j * LANES:(j + 1) * LANES], cos_ref[...], sin_ref[...], half)
            mix_ref[:, c0 + j * LANES:c0 + (j + 1) * LANES] = t.astype(BF16)

    for c0 in range(OFF_QR, OFF_KR, MXU_N):
        rot_store(c0, crq, srq, RET_QK_DIM // 2)
    for c0 in range(OFF_KR, OFF_VR, MXU_N):
        rot_store(c0, crk, srk, RET_QK_DIM // 2)
    for c0 in range(OFF_VR, OFF_GR, MXU_N):
        mix_ref[:, c0:c0 + MXU_N] = proj(c0, MXU_N).astype(BF16)
    for c0 in range(OFF_GR, OFF_QA, MXU_N):
        p = proj(c0, MXU_N)
        mix_ref[:, c0:c0 + MXU_N] = (p * _sigmoid(p)).astype(BF16)
    for c0 in range(OFF_QA, OFF_KA, MXU_N):
        rot_store(c0, caq, saq, ROPE_DIM // 2)
    p = proj(OFF_KA, MXU_N)
    mix_ref[:, OFF_KA:OFF_VA] = _rotate(p[:, :LANES], cak[...], sak[...], ROPE_DIM // 2).astype(BF16)
    mix_ref[:, OFF_VA:MIX_COLS] = p[:, LANES:].astype(BF16)
    for c0 in range(0, GATE_COLS, MXU_N):
        gate_ref[:, c0:c0 + MXU_N] = _sigmoid(proj(MIX_COLS + c0, MXU_N)).astype(BF16)


def _in_proj(h, gain, w, tables, tiles_per_row):
    t = h.shape[0]
    tok = lambda cols: pl.BlockSpec((IN_TM, cols), lambda i: (i, 0))
    tab = pl.BlockSpec((IN_TM, LANES), lambda i: (i % tiles_per_row, 0))
    return pl.pallas_call(
        _in_proj_kernel,
        grid=(t // IN_TM,),
        in_specs=[tok(D_MODEL), _resident((1, D_MODEL)), _resident((D_MODEL, D_IN))] + [tab] * 8,
        out_specs=[tok(MIX_COLS), tok(GATE_COLS)],
        out_shape=[jax.ShapeDtypeStruct((t, MIX_COLS), BF16), jax.ShapeDtypeStruct((t, GATE_COLS), BF16)],
        compiler_params=pltpu.CompilerParams(dimension_semantics=("arbitrary",), vmem_limit_bytes=VMEM_LIMIT),
        name="in_proj",
    )(h, gain, w, *tables)


def _ret_kernel(rd_ref, q_ref, k_ref, v_ref, sg_ref, o_ref, tab_scr, sf_scr, s_scr, *, n_chunks):
    C = CHUNK
    seq = n_chunks * C
    lg = -jnp.exp(rd_ref[...])
    lgf_a, lgf_b, lgb_a, lgb_b, lgf_p, lgb_p = [lg[i:i + 1, :] for i in range(6)]

    row = lax.broadcasted_iota(jnp.int32, (C, LANES), 0)
    col = lax.broadcasted_iota(jnp.int32, (C, LANES), 1)
    a = row.astype(F32)
    rel = (row - col).astype(F32)

    def decay_matrix(lf, lb):
        return jnp.where(rel >= 0, jnp.exp(lf * jnp.maximum(rel, 0.0)), jnp.exp(lb * jnp.maximum(-rel, 0.0)))

    tab_scr[0] = jnp.exp(lgf_p * (a + 1.0))
    tab_scr[1] = jnp.exp(lgb_p * (C - a))
    tab_scr[2] = jnp.exp(lgf_p * (C - 1.0 - a))
    tab_scr[3] = jnp.exp(lgb_p * a)
    tab_scr[4] = decay_matrix(lgf_a, lgb_a)
    tab_scr[5] = decay_matrix(lgf_b, lgb_b)
    top = row < RET_QK_DIM
    gf_rows = jnp.where(top, jnp.exp(lgf_a * C), jnp.exp(lgf_b * C))
    gb_rows = jnp.where(top, jnp.exp(lgb_a * C), jnp.exp(lgb_b * C))
    gf_rows = jnp.concatenate([gf_rows, gf_rows], axis=1)
    gb_rows = jnp.concatenate([gb_rows, gb_rows], axis=1)
    zero_tile = jnp.zeros((C, LANES), F32)
    lo = col < RET_QK_DIM

    def block_diag(s):
        return jnp.concatenate([jnp.where(top, s[:, :LANES], 0.0), jnp.where(top, 0.0, s[:, LANES:])], axis=1)

    def kv(kc, vc, tab):
        kt = (kc.astype(F32) * tab_scr[tab]).T.astype(BF16)
        return _dot(kt, vc)

    def emit(qc, kc, vc, sgc, sf, sb):
        qf = qc.astype(F32)
        s_a = _dot_nt(jnp.where(lo, qf, 0.0).astype(BF16), kc)
        s_b = _dot_nt(jnp.where(lo, 0.0, qf).astype(BF16), kc)
        p_a = (s_a * tab_scr[4]).astype(BF16)
        p_b = (s_b * tab_scr[5]).astype(BF16)
        o = jnp.concatenate([_dot(p_a, vc[:, :LANES]), _dot(p_b, vc[:, LANES:])], axis=1)
        o = o + _dot((qf * tab_scr[0]).astype(BF16), sf)
        o = o + _dot((qf * tab_scr[1]).astype(BF16), block_diag(sb).astype(BF16))
        halves = []
        for j in range(2):
            oh = o[:, j * LANES:(j + 1) * LANES]
            mu = jnp.mean(oh, axis=-1, keepdims=True)
            var = jnp.mean(jnp.square(oh - mu), axis=-1, keepdims=True)
            halves.append((oh - mu) * lax.rsqrt(var + EPS))
        return (jnp.concatenate(halves, axis=1) * sgc.astype(F32)).astype(BF16)

    def padded(ref):
        m = ref[seq:seq + N_META, :]
        return jnp.concatenate([jnp.zeros((C - N_META, m.shape[1]), m.dtype), m], axis=0)

    def seq_rows(c):
        return pl.ds(pl.multiple_of(c * C, C), C)

    km, vm = padded(k_ref), padded(v_ref)

    sf_scr[0] = jnp.zeros((C, 2 * LANES), BF16)
    s_scr[...] = kv(km, vm, 2)

    def fwd(c, carry):
        s = s_scr[...]
        sf_scr[c + 1] = block_diag(s).astype(BF16)
        r = seq_rows(c)
        s_scr[...] = s * gf_rows + kv(k_ref[r, :], v_ref[r, :], 2)
        return carry

    lax.fori_loop(0, n_chunks - 1, fwd, 0)
    sf_scr[n_chunks] = block_diag(s_scr[...]).astype(BF16)

    s_scr[...] = jnp.zeros((C, 2 * LANES), F32)

    def bwd(i, carry):
        c = n_chunks - 1 - i
        r = seq_rows(c)
        qc, kc, vc = q_ref[r, :], k_ref[r, :], v_ref[r, :]
        s = s_scr[...]
        o_ref[r, :] = emit(qc, kc, vc, sg_ref[r, :], sf_scr[c + 1], s)
        s_scr[...] = s * gb_rows + kv(kc, vc, 3)
        return carry

    lax.fori_loop(0, n_chunks, bwd, 0)
    om = emit(padded(q_ref), km, vm, padded(sg_ref), sf_scr[0], s_scr[...])
    o_ref[seq:seq + N_META, :] = om[C - N_META:, :]


def _retention(mix, rd_lanes, batch, ltok):
    n_chunks = (ltok - N_META) // CHUNK
    pairs = RET_HEADS // 2
    blk = lambda cols, base: pl.BlockSpec((ltok, cols), lambda b, j: (b, base + j))
    return pl.pallas_call(
        functools.partial(_ret_kernel, n_chunks=n_chunks),
        grid=(batch, pairs),
        in_specs=[
            pl.BlockSpec((None, 8, LANES), lambda b, j: (j, 0, 0)),
            blk(LANES, OFF_QR // LANES),
            blk(LANES, OFF_KR // LANES),
            blk(2 * LANES, OFF_VR // (2 * LANES)),
            blk(2 * LANES, OFF_GR // (2 * LANES)),
        ],
        out_specs=pl.BlockSpec((ltok, 2 * LANES), lambda b, j: (b, j)),
        out_shape=jax.ShapeDtypeStruct((batch * ltok, RET_V), BF16),
        scratch_shapes=[
            pltpu.VMEM((6, CHUNK, LANES), F32),
            pltpu.VMEM((n_chunks + 1, CHUNK, 2 * LANES), BF16),
            pltpu.VMEM((CHUNK, 2 * LANES), F32),
        ],
        compiler_params=pltpu.CompilerParams(dimension_semantics=("arbitrary", "arbitrary"),
                                             vmem_limit_bytes=VMEM_LIMIT),
        name="retention",
    )(rd_lanes, mix, mix, mix, mix)


def _decay_lanes(ret_decay_l):
    rd = ret_decay_l.astype(F32)
    pairs = RET_HEADS // 2
    fa, fb = rd[0, 0::2], rd[0, 1::2]
    ba, bb = rd[1, 0::2], rd[1, 1::2]
    full = lambda v: jnp.broadcast_to(v[:, None], (pairs, LANES))
    split = lambda x, y: jnp.concatenate([jnp.broadcast_to(x[:, None], (pairs, RET_QK_DIM)),
                                          jnp.broadcast_to(y[:, None], (pairs, RET_QK_DIM))], axis=1)
    rows = [full(fa), full(fb), full(ba), full(bb), split(fa, fb), split(ba, bb), full(fa), full(fa)]
    return jnp.stack(rows, axis=1)


BAND = 3 * CHUNK


def _att_kernel(sink_ref, q_ref, k_ref, v_ref, o_ref, ke_scr, ko_scr, ve_scr, vo_scr, *, n_blocks):
    C = CHUNK
    seq = n_blocks * C
    g = pl.program_id(1)
    ltok = seq + N_META
    lane = lax.broadcasted_iota(jnp.int32, (ltok, LANES), 1)
    lo = lane < ATT_HEAD_DIM
    first = g == 0

    def own_head_low(ref):
        x = ref[...].astype(F32)
        return jnp.where(first, x, pltpu.roll(x, ATT_HEAD_DIM, 1))

    kk = own_head_low(k_ref)
    vv = own_head_low(v_ref)
    pad_rows = C - N_META
    for scr, val in ((ke_scr, jnp.where(lo, kk, 0.0)), (ko_scr, jnp.where(lo, 0.0, pltpu.roll(kk, ATT_HEAD_DIM, 1)))):
        scr[0:ltok, :] = val.astype(BF16)
        scr[ltok:ltok + pad_rows, :] = jnp.zeros((pad_rows, LANES), BF16)
    ones_lo = jnp.where(lo, 1.0, 0.0)
    ve = jnp.concatenate([jnp.where(lo, vv, 0.0), ones_lo], axis=1)
    vo = jnp.concatenate([jnp.where(lo, 0.0, pltpu.roll(vv, ATT_HEAD_DIM, 1)), 1.0 - ones_lo], axis=1)
    for scr, val in ((ve_scr, ve), (vo_scr, vo)):
        scr[0:ltok, :] = val.astype(BF16)
        scr[ltok:ltok + pad_rows, :] = jnp.zeros((pad_rows, 2 * LANES), BF16)

    meta_ok = lax.broadcasted_iota(jnp.int32, (1, C), 1) < N_META

    def block(qrows, start, off):
        rows = qrows.shape[0]
        d = (lax.broadcasted_iota(jnp.int32, (rows, BAND), 0)
             - lax.broadcasted_iota(jnp.int32, (rows, BAND), 1) + off)
        band_ok = jnp.abs(d) <= C
        lo_r = lax.broadcasted_iota(jnp.int32, (rows, LANES), 1) < ATT_HEAD_DIM
        band = pl.ds(start, BAND)
        meta = pl.ds(seq, C)
        outs = []
        for p in range(ATT_GROUP // 2):
            qp = qrows[:, p * LANES:(p + 1) * LANES]
            res, sink_terms = [], []
            for par, (k_scr, v_scr) in enumerate(((ke_scr, ve_scr), (ko_scr, vo_scr))):
                sink = sink_ref[g * ATT_GROUP + 2 * p + par]
                sb = jnp.where(band_ok, _dot_nt(qp, k_scr[band, :]), NEG_INF)
                sm = jnp.where(meta_ok, _dot_nt(qp, k_scr[meta, :]), NEG_INF)
                m = jnp.maximum(jnp.maximum(jnp.max(sb, axis=-1, keepdims=True),
                                            jnp.max(sm, axis=-1, keepdims=True)), sink)
                eb = jnp.exp(sb - m).astype(BF16)
                em = jnp.exp(sm - m).astype(BF16)
                res.append(_dot(eb, v_scr[band, :]) + _dot(em, v_scr[meta, :]))
                sink_terms.append(jnp.exp(sink - m))
            tot = res[0] + res[1]
            den = tot[:, LANES:] + jnp.where(lo_r, sink_terms[0], sink_terms[1])
            outs.append(tot[:, :LANES] / den)
        return jnp.concatenate(outs, axis=1).astype(BF16)

    def body(n, carry):
        start = jnp.clip((n - 1) * C, 0, seq - BAND)
        start = pl.multiple_of(start, C)
        r = pl.ds(pl.multiple_of(n * C, C), C)
        o_ref[r, :] = block(q_ref[r, :], start, n * C - start)
        return carry

    lax.fori_loop(0, n_blocks, body, 0)
    o_ref[seq:ltok, :] = block(q_ref[seq:ltok, :], 0, -N_META)


def _attention(mix, sink, batch, ltok):
    n_blocks = (ltok - N_META) // CHUNK
    padded = ltok + CHUNK - N_META
    return pl.pallas_call(
        functools.partial(_att_kernel, n_blocks=n_blocks),
        grid=(batch, ATT_KV_HEADS),
        in_specs=[
            pl.BlockSpec(memory_space=pltpu.SMEM),
            pl.BlockSpec((ltok, 2 * LANES), lambda b, g: (b, OFF_QA // (2 * LANES) + g)),
            pl.BlockSpec((ltok, LANES), lambda b, g: (b, OFF_KA // LANES)),
            pl.BlockSpec((ltok, LANES), lambda b, g: (b, OFF_VA // LANES)),
        ],
        out_specs=pl.BlockSpec((ltok, 2 * LANES), lambda b, g: (b, g)),
        out_shape=jax.ShapeDtypeStruct((batch * ltok, ATT_Q), BF16),
        scratch_shapes=[
            pltpu.VMEM((padded, LANES), BF16),
            pltpu.VMEM((padded, LANES), BF16),
            pltpu.VMEM((padded, 2 * LANES), BF16),
            pltpu.VMEM((padded, 2 * LANES), BF16),
        ],
        compiler_params=pltpu.CompilerParams(dimension_semantics=("arbitrary", "arbitrary"),
                                             vmem_limit_bytes=VMEM_LIMIT),
        name="attention",
    )(sink, mix, mix, mix)


POST_TM = 344
FF_CHUNK = 1024


def _post_ffn_kernel(h_ref, yr_ref, ya_ref, gt_ref, wr_ref, wa_ref, wm_ref, w1_ref, w2_ref,
                     gpost_ref, gpre_ref, gffpost_ref, o_ref):
    y_r = _dot(yr_ref[...], wr_ref[...])
    y_a = _dot(ya_ref[...], wa_ref[...])
    z = gt_ref[:, :D_MODEL].astype(F32) * y_r + gt_ref[:, D_MODEL:].astype(F32) * y_a
    mix = _dot(z.astype(BF16), wm_ref[...])
    h1 = h_ref[...] + _rms(mix, gpost_ref[...])
    u = _rms(h1, gpre_ref[...]).astype(BF16)
    ff = jnp.zeros(h1.shape, F32)
    for c0 in range(0, D_FF, FF_CHUNK):
        hid = jnp.maximum(_dot(u, w1_ref[:, c0:c0 + FF_CHUNK]), 0.0)
        ff = ff + _dot((hid * hid).astype(BF16), w2_ref[c0:c0 + FF_CHUNK, :])
    o_ref[...] = h1 + _rms(ff, gffpost_ref[...])


def _post_ffn(h, y_r, y_a, gates, wr, wa, wm, w1, w2, gpost, gpre, gffpost):
    t = h.shape[0]
    tok = lambda cols: pl.BlockSpec((POST_TM, cols), lambda i: (i, 0))
    return pl.pallas_call(
        _post_ffn_kernel,
        grid=(t // POST_TM,),
        in_specs=[tok(D_MODEL), tok(RET_V), tok(ATT_Q), tok(GATE_COLS),
                  _resident((RET_V, D_MODEL)), _resident((ATT_Q, D_MODEL)), _resident((D_MODEL, D_MODEL)),
                  _resident((D_MODEL, D_FF)), _resident((D_FF, D_MODEL)),
                  _resident((1, D_MODEL)), _resident((1, D_MODEL)), _resident((1, D_MODEL))],
        out_specs=tok(D_MODEL),
        out_shape=jax.ShapeDtypeStruct((t, D_MODEL), F32),
        compiler_params=pltpu.CompilerParams(dimension_semantics=("arbitrary",), vmem_limit_bytes=VMEM_LIMIT),
        name="post_ffn",
    )(h, y_r, y_a, gates, wr, wa, wm, w1, w2, gpost, gpre, gffpost)


def kernel(x, meta_tokens, w_in, w_ret_o, w_att_o, w_mix_o, w_ff1, w_ff2,
           norm_mix_pre, norm_mix_post, norm_ff_pre, norm_ff_post, ret_decay, attn_sink):
    batch, seq, d = x.shape
    depth = w_in.shape[0]
    ltok = seq + N_META
    assert d == D_MODEL and seq % CHUNK == 0 and seq >= BAND and ltok % IN_TM == 0 and ltok % POST_TM == 0
    meta = jnp.broadcast_to(meta_tokens[None].astype(x.dtype), (batch, N_META, d))
    h = jnp.concatenate([x, meta], axis=1).reshape(batch * ltok, d)
    tables = _rotary_tables(seq)
    w_in_b, w_ret_b, w_att_b, w_mix_b, w_ff1_b, w_ff2_b = [
        w.astype(BF16) for w in (w_in, w_ret_o, w_att_o, w_mix_o, w_ff1, w_ff2)]
    row = lambda g: g.astype(F32).reshape(1, d)
    for l in range(depth):
        mix, gates = _in_proj(h, row(norm_mix_pre[l]), w_in_b[l], tables, ltok // IN_TM)
        y_r = _retention(mix, _decay_lanes(ret_decay[l]), batch, ltok)
        y_a = _attention(mix, attn_sink[l].astype(F32), batch, ltok)
        h = _post_ffn(h, y_r, y_a, gates, w_ret_b[l], w_att_b[l], w_mix_b[l], w_ff1_b[l], w_ff2_b[l],
                      row(norm_mix_post[l]), row(norm_ff_pre[l]), row(norm_ff_post[l]))
    return h.reshape(batch, ltok, d)[:, :seq]
```

```python
import functools

import jax
import jax.numpy as jnp
from jax import lax
from jax.experimental import pallas as pl
from jax.experimental.pallas import tpu as pltpu

F32 = jnp.float32
BF16 = jnp.bfloat16

D_MODEL = 1024
N_META = 16
CHUNK = 128
RET_HEADS = 8
RET_QK_DIM = 64
RET_V_DIM = 128
ATT_Q_HEADS = 8
ATT_KV_HEADS = 2
ATT_GROUP = ATT_Q_HEADS // ATT_KV_HEADS
ATT_HEAD_DIM = 64
ROPE_DIM = 16
ROPE_THETA = 500000.0
XPOS_THETA = 10000.0
D_FF = 4 * D_MODEL
EPS = 1e-6
NEG_INF = -1e30

RET_QK = RET_HEADS * RET_QK_DIM
RET_V = RET_HEADS * RET_V_DIM
ATT_Q = ATT_Q_HEADS * ATT_HEAD_DIM
ATT_KV = ATT_KV_HEADS * ATT_HEAD_DIM
OFF_QR = 0
OFF_KR = OFF_QR + RET_QK
OFF_VR = OFF_KR + RET_QK
OFF_GR = OFF_VR + RET_V
OFF_QA = OFF_GR + RET_V
OFF_KA = OFF_QA + ATT_Q
OFF_VA = OFF_KA + ATT_KV
MIX_COLS = OFF_VA + ATT_KV
GATE_COLS = 2 * D_MODEL
D_IN = MIX_COLS + GATE_COLS

LANES = 128
MXU_N = 256
VMEM_LIMIT = 56 * 1024 * 1024


def _dot(a, b):
    return jnp.dot(a, b, preferred_element_type=F32)


def _dot_nt(a, b):
    return lax.dot_general(a, b, (((1,), (1,)), ((), ())), preferred_element_type=F32)


def _rms(x, g):
    return x * lax.rsqrt(jnp.mean(x * x, axis=-1, keepdims=True) + EPS) * g


def _sigmoid(x):
    return 1.0 / (1.0 + jnp.exp(-x))


def _layer_resident(shape, layer):
    return pl.BlockSpec((None,) + shape, lambda *_: (layer,) + (0,) * len(shape), pipeline_mode=pl.Buffered(1))


def _rotary_tables(seq):
    pos = jnp.concatenate([jnp.arange(N_META, N_META + seq), jnp.arange(N_META)]).astype(F32)
    d = jnp.arange(LANES) % RET_QK_DIM

    def table(theta, rot_dim):
        half = rot_dim // 2
        freqs = jnp.power(jnp.float32(theta), -jnp.arange(0, rot_dim, 2, dtype=F32) / rot_dim)
        ang = pos[:, None] * freqs[None, :]
        cos, sin = jnp.cos(ang), jnp.sin(ang)
        idx = d % half
        rotated = (d < rot_dim)[None, :]
        cos_l = jnp.where(rotated, cos[:, idx], 1.0)
        sin_l = jnp.where(rotated, jnp.where((d < half)[None, :], -sin[:, idx], sin[:, idx]), 0.0)
        return cos_l, sin_l

    cr, sr = table(XPOS_THETA, RET_QK_DIM)
    ca, sa = table(ROPE_THETA, ROPE_DIM)
    qs_r = RET_QK_DIM ** -0.5
    qs_a = ATT_HEAD_DIM ** -0.5
    return (cr * qs_r, sr * qs_r, cr, sr, ca * qs_a, sa * qs_a, ca, sa)


def _rotate(x, cos, sin, half):
    lane = lax.broadcasted_iota(jnp.int32, x.shape, 1) % RET_QK_DIM
    partner = jnp.where(lane < half, pltpu.roll(x, LANES - half, 1), pltpu.roll(x, half, 1))
    return x * cos + partner * sin


IN_TM = 688


def _in_proj_kernel(h_ref, g_ref, w_ref, crq, srq, crk, srk, caq, saq, cak, sak, mix_ref, gate_ref):
    x = h_ref[...]
    u = _rms(x, g_ref[...]).astype(BF16)

    def proj(c0, n):
        return _dot(u, w_ref[:, c0:c0 + n])

    def rot_store(c0, cos_ref, sin_ref, half):
        p = proj(c0, MXU_N)
        for j in range(MXU_N // LANES):
            t = _rotate(p[:, j * LANES:(j + 1) * LANES], cos_ref[...], sin_ref[...], half)
            mix_ref[:, c0 + j * LANES:c0 + (j + 1) * LANES] = t.astype(BF16)

    for c0 in range(OFF_QR, OFF_KR, MXU_N):
        rot_store(c0, crq, srq, RET_QK_DIM // 2)
    for c0 in range(OFF_KR, OFF_VR, MXU_N):
        rot_store(c0, crk, srk, RET_QK_DIM // 2)
    for c0 in range(OFF_VR, OFF_GR, MXU_N):
        mix_ref[:, c0:c0 + MXU_N] = proj(c0, MXU_N).astype(BF16)
    for c0 in range(OFF_GR, OFF_QA, MXU_N):
        p = proj(c0, MXU_N)
        mix_ref[:, c0:c0 + MXU_N] = (p * _sigmoid(p)).astype(BF16)
    for c0 in range(OFF_QA, OFF_KA, MXU_N):
        rot_store(c0, caq, saq, ROPE_DIM // 2)
    p = proj(OFF_KA, MXU_N)
    mix_ref[:, OFF_KA:OFF_VA] = _rotate(p[:, :LANES], cak[...], sak[...], ROPE_DIM // 2).astype(BF16)
    mix_ref[:, OFF_VA:MIX_COLS] = p[:, LANES:].astype(BF16)
    for c0 in range(0, GATE_COLS, MXU_N):
        gate_ref[:, c0:c0 + MXU_N] = _sigmoid(proj(MIX_COLS + c0, MXU_N)).astype(BF16)


def _in_proj(h, gains, w, tables, layer, tiles_per_row):
    t = h.shape[0]
    tok = lambda cols: pl.BlockSpec((IN_TM, cols), lambda i: (i, 0))
    tab = pl.BlockSpec((IN_TM, LANES), lambda i: (i % tiles_per_row, 0))
    return pl.pallas_call(
        _in_proj_kernel,
        grid=(t // IN_TM,),
        in_specs=[tok(D_MODEL), _layer_resident((1, D_MODEL), layer), _layer_resident((D_MODEL, D_IN), layer)]
                 + [tab] * 8,
        out_specs=[tok(MIX_COLS), tok(GATE_COLS)],
        out_shape=[jax.ShapeDtypeStruct((t, MIX_COLS), BF16), jax.ShapeDtypeStruct((t, GATE_COLS), BF16)],
        compiler_params=pltpu.CompilerParams(dimension_semantics=("arbitrary",), vmem_limit_bytes=VMEM_LIMIT),
        name="in_proj",
    )(h, gains, w, *tables)


def _ret_kernel(rd_ref, q_ref, k_ref, v_ref, sg_ref, o_ref, tab_scr, dec_scr, pad_scr, padw_scr,
                kv_scr, st_scr, s_scr, *, n_chunks):
    C = CHUNK
    seq = n_chunks * C
    W = 2 * LANES
    H = RET_QK_DIM
    lg = -jnp.exp(rd_ref[...])
    lgf_a, lgf_b, lgb_a, lgb_b, lgf_p, lgb_p = [lg[i:i + 1, :] for i in range(6)]

    row = lax.broadcasted_iota(jnp.int32, (C, LANES), 0)
    col = lax.broadcasted_iota(jnp.int32, (C, LANES), 1)
    a = row.astype(F32)
    rel = (row - col).astype(F32)

    def decay_matrix(lf, lb):
        return jnp.where(rel >= 0, jnp.exp(lf * jnp.maximum(rel, 0.0)), jnp.exp(lb * jnp.maximum(-rel, 0.0)))

    tab_scr[0] = jnp.concatenate([jnp.exp(lgf_p * (a + 1.0)), jnp.exp(lgb_p * (C - a))], axis=1)
    tab_scr[1] = jnp.concatenate([jnp.exp(lgf_p * (C - 1.0 - a)), jnp.exp(lgb_p * a)], axis=1)
    tab_scr[2] = jnp.concatenate([decay_matrix(lgf_a, lgb_a), decay_matrix(lgf_b, lgb_b)], axis=1)
    top = row < H
    for i, (la, lb) in enumerate(((lgf_a, lgf_b), (lgb_a, lgb_b))):
        g = jnp.where(top, jnp.exp(la * C), jnp.exp(lb * C))
        dec_scr[i] = jnp.concatenate([g, g], axis=1)
    lo = col < H

    pad_rows = C - N_META
    for i, ref in enumerate((q_ref, k_ref)):
        pad_scr[i, 0:pad_rows, :] = jnp.zeros((pad_rows, LANES), BF16)
        pad_scr[i, pad_rows:C, :] = ref[seq:seq + N_META, :]
    for i, ref in enumerate((v_ref, sg_ref)):
        padw_scr[i, 0:pad_rows, :] = jnp.zeros((pad_rows, W), BF16)
        padw_scr[i, pad_rows:C, :] = ref[seq:seq + N_META, :]

    def seq_rows(c):
        return pl.ds(c * C, C)

    def kv(kc, vc):
        kf = kc.astype(F32)
        ks = jnp.concatenate([kf, kf], axis=1) * tab_scr[1]
        kt = jnp.concatenate([ks[:, :LANES].T, ks[:, LANES:].T], axis=0).astype(BF16)
        x = _dot(kt, vc)
        z = jnp.zeros((H, LANES), F32)
        rows = []
        for d in range(2):
            rows.append(jnp.concatenate([x[d * C:d * C + H, :LANES], z], axis=1))
            rows.append(jnp.concatenate([z, x[d * C + H:(d + 1) * C, LANES:]], axis=1))
        return jnp.concatenate(rows, axis=0)

    kv_scr[0] = kv(pad_scr[1], padw_scr[0])
    for c in range(n_chunks):
        r = seq_rows(c)
        kv_scr[c + 1] = kv(k_ref[r, :], v_ref[r, :])

    s_scr[...] = jnp.zeros((2, C, W), F32)
    for t in range(n_chunks + 1):
        u = n_chunks - t
        sf = s_scr[0]
        sb = s_scr[1]
        st_scr[t, 0:C, :] = sf.astype(BF16)
        st_scr[u, C:2 * C, :] = sb.astype(BF16)
        s_scr[0] = sf * dec_scr[0] + kv_scr[t, 0:C, :]
        s_scr[1] = sb * dec_scr[1] + kv_scr[u, C:2 * C, :]

    def emit(qc, kc, vc, sgc, st):
        qf = qc.astype(F32)
        kf = kc.astype(F32)
        k2 = jnp.concatenate([jnp.where(lo, kf, 0.0), jnp.where(lo, 0.0, kf)], axis=0).astype(BF16)
        p = (_dot_nt(qc, k2) * tab_scr[2]).astype(BF16)
        zb = jnp.zeros((C, LANES), BF16)
        vbd = jnp.concatenate([jnp.concatenate([vc[:, :LANES], zb], axis=1),
                               jnp.concatenate([zb, vc[:, LANES:]], axis=1)], axis=0)
        qs = (jnp.concatenate([qf, qf], axis=1) * tab_scr[0]).astype(BF16)
        o = _dot(p, vbd) + _dot(qs, st)
        halves = []
        for j in range(2):
            oh = o[:, j * LANES:(j + 1) * LANES]
            mu = jnp.mean(oh, axis=-1, keepdims=True)
            var = jnp.mean(jnp.square(oh - mu), axis=-1, keepdims=True)
            halves.append((oh - mu) * lax.rsqrt(var + EPS))
        return (jnp.concatenate(halves, axis=1) * sgc.astype(F32)).astype(BF16)

    for c in range(n_chunks):
        r = seq_rows(c)
        o_ref[r, :] = emit(q_ref[r, :], k_ref[r, :], v_ref[r, :], sg_ref[r, :], st_scr[c + 1])
    om = emit(pad_scr[0], pad_scr[1], padw_scr[0], padw_scr[1], st_scr[0])
    o_ref[seq:seq + N_META, :] = om[pad_rows:, :]


def _retention(mix, rd_lanes, batch, ltok):
    n_chunks = (ltok - N_META) // CHUNK
    pairs = RET_HEADS // 2
    blk = lambda cols, base: pl.BlockSpec((ltok, cols), lambda b, j: (b, base + j))
    return pl.pallas_call(
        functools.partial(_ret_kernel, n_chunks=n_chunks),
        grid=(batch, pairs),
        in_specs=[
            pl.BlockSpec((None, 8, LANES), lambda b, j: (j, 0, 0)),
            blk(LANES, OFF_QR // LANES),
            blk(LANES, OFF_KR // LANES),
            blk(2 * LANES, OFF_VR // (2 * LANES)),
            blk(2 * LANES, OFF_GR // (2 * LANES)),
        ],
        out_specs=pl.BlockSpec((ltok, 2 * LANES), lambda b, j: (b, j)),
        out_shape=jax.ShapeDtypeStruct((batch * ltok, RET_V), BF16),
        scratch_shapes=[
            pltpu.VMEM((3, CHUNK, 2 * LANES), F32),
            pltpu.VMEM((2, CHUNK, 2 * LANES), F32),
            pltpu.VMEM((2, CHUNK, LANES), BF16),
            pltpu.VMEM((2, CHUNK, 2 * LANES), BF16),
            pltpu.VMEM((n_chunks + 1, 2 * CHUNK, 2 * LANES), F32),
            pltpu.VMEM((n_chunks + 1, 2 * CHUNK, 2 * LANES), BF16),
            pltpu.VMEM((2, CHUNK, 2 * LANES), F32),
        ],
        compiler_params=pltpu.CompilerParams(dimension_semantics=("arbitrary", "arbitrary"),
                                             vmem_limit_bytes=VMEM_LIMIT),
        name="retention",
    )(rd_lanes, mix, mix, mix, mix)


def _decay_lanes(ret_decay_l):
    rd = ret_decay_l.astype(F32)
    pairs = RET_HEADS // 2
    fa, fb = rd[0, 0::2], rd[0, 1::2]
    ba, bb = rd[1, 0::2], rd[1, 1::2]
    full = lambda v: jnp.broadcast_to(v[:, None], (pairs, LANES))
    split = lambda x, y: jnp.concatenate([jnp.broadcast_to(x[:, None], (pairs, RET_QK_DIM)),
                                          jnp.broadcast_to(y[:, None], (pairs, RET_QK_DIM))], axis=1)
    rows = [full(fa), full(fb), full(ba), full(bb), split(fa, fb), split(ba, bb), full(fa), full(fa)]
    return jnp.stack(rows, axis=1)


BAND = 3 * CHUNK
KEYS = BAND + CHUNK


def _att_kernel(sink_ref, q_ref, k_ref, v_ref, o_ref, k_scr, v_scr, *, n_blocks):
    C = CHUNK
    seq = n_blocks * C
    g = pl.program_id(1)
    ltok = seq + N_META
    lane = lax.broadcasted_iota(jnp.int32, (ltok, LANES), 1)
    lo = lane < ATT_HEAD_DIM
    first = g == 0

    def own_head_low(ref):
        x = ref[...].astype(F32)
        return jnp.where(first, x, pltpu.roll(x, ATT_HEAD_DIM, 1))

    kk = own_head_low(k_ref)
    vv = own_head_low(v_ref)
    pad_rows = C - N_META
    ones_lo = jnp.where(lo, 1.0, 0.0)
    k_par = (jnp.where(lo, kk, 0.0), jnp.where(lo, 0.0, pltpu.roll(kk, ATT_HEAD_DIM, 1)))
    v_par = (jnp.concatenate([jnp.where(lo, vv, 0.0), ones_lo], axis=1),
             jnp.concatenate([jnp.where(lo, 0.0, pltpu.roll(vv, ATT_HEAD_DIM, 1)), 1.0 - ones_lo], axis=1))
    for par in range(2):
        k_scr[par, 0:ltok, :] = k_par[par].astype(BF16)
        k_scr[par, ltok:ltok + pad_rows, :] = jnp.zeros((pad_rows, LANES), BF16)
        v_scr[par, 0:ltok, :] = v_par[par].astype(BF16)
        v_scr[par, ltok:ltok + pad_rows, :] = jnp.zeros((pad_rows, 2 * LANES), BF16)

    key_col = lax.broadcasted_iota(jnp.int32, (1, KEYS), 1)
    in_band = key_col < BAND
    reach = jnp.where(in_band, C, jnp.where(key_col < BAND + N_META, 1 << 20, -1))

    def block(qrows, start, off):
        rows = qrows.shape[0]
        centre = jnp.where(in_band, key_col - off, 0)
        ok = jnp.abs(lax.broadcasted_iota(jnp.int32, (rows, KEYS), 0) - centre) <= reach
        lo_r = lax.broadcasted_iota(jnp.int32, (rows, LANES), 1) < ATT_HEAD_DIM
        band = pl.ds(start, BAND)
        meta = pl.ds(seq, C)
        outs = []
        for p in range(ATT_GROUP // 2):
            qp = qrows[:, p * LANES:(p + 1) * LANES]
            res, sink_terms = [], []
            for par in range(2):
                sink = sink_ref[g * ATT_GROUP + 2 * p + par]
                keys = jnp.concatenate([k_scr[par, band, :], k_scr[par, meta, :]], axis=0)
                vals = jnp.concatenate([v_scr[par, band, :], v_scr[par, meta, :]], axis=0)
                s = jnp.where(ok, _dot_nt(qp, keys), NEG_INF)
                m = jnp.maximum(jnp.max(s, axis=-1, keepdims=True), sink)
                res.append(_dot(jnp.exp(s - m).astype(BF16), vals))
                sink_terms.append(jnp.exp(sink - m))
            tot = res[0] + res[1]
            den = tot[:, LANES:] + jnp.where(lo_r, sink_terms[0], sink_terms[1])
            outs.append(tot[:, :LANES] / den)
        return jnp.concatenate(outs, axis=1).astype(BF16)

    def body(n, carry):
        start = jnp.clip((n - 1) * C, 0, seq - BAND)
        start = pl.multiple_of(start, C)
        r = pl.ds(pl.multiple_of(n * C, C), C)
        o_ref[r, :] = block(q_ref[r, :], start, n * C - start)
        return carry

    lax.fori_loop(0, n_blocks, body, 0, unroll=2)
    o_ref[seq:ltok, :] = block(q_ref[seq:ltok, :], 0, -N_META)


def _attention(mix, sink, batch, ltok):
    n_blocks = (ltok - N_META) // CHUNK
    padded = ltok + CHUNK - N_META
    return pl.pallas_call(
        functools.partial(_att_kernel, n_blocks=n_blocks),
        grid=(batch, ATT_KV_HEADS),
        in_specs=[
            pl.BlockSpec(memory_space=pltpu.SMEM),
            pl.BlockSpec((ltok, 2 * LANES), lambda b, g: (b, OFF_QA // (2 * LANES) + g)),
            pl.BlockSpec((ltok, LANES), lambda b, g: (b, OFF_KA // LANES)),
            pl.BlockSpec((ltok, LANES), lambda b, g: (b, OFF_VA // LANES)),
        ],
        out_specs=pl.BlockSpec((ltok, 2 * LANES), lambda b, g: (b, g)),
        out_shape=jax.ShapeDtypeStruct((batch * ltok, ATT_Q), BF16),
        scratch_shapes=[
            pltpu.VMEM((2, padded, LANES), BF16),
            pltpu.VMEM((2, padded, 2 * LANES), BF16),
        ],
        compiler_params=pltpu.CompilerParams(dimension_semantics=("arbitrary", "arbitrary"),
                                             vmem_limit_bytes=VMEM_LIMIT),
        name="attention",
    )(sink, mix, mix, mix)


POST_TM = 344
FF_CHUNK = 1024


def _post_ffn_kernel(h_ref, yr_ref, ya_ref, gt_ref, wr_ref, wa_ref, wm_ref, w1_ref, w2_ref,
                     gpost_ref, gpre_ref, gffpost_ref, o_ref):
    y_r = _dot(yr_ref[...], wr_ref[...])
    y_a = _dot(ya_ref[...], wa_ref[...])
    z = gt_ref[:, :D_MODEL].astype(F32) * y_r + gt_ref[:, D_MODEL:].astype(F32) * y_a
    mix = _dot(z.astype(BF16), wm_ref[...])
    h1 = h_ref[...] + _rms(mix, gpost_ref[...])
    u = _rms(h1, gpre_ref[...]).astype(BF16)
    ff = jnp.zeros(h1.shape, F32)
    for c0 in range(0, D_FF, FF_CHUNK):
        hid = jnp.maximum(_dot(u, w1_ref[:, c0:c0 + FF_CHUNK]), 0.0)
        ff = ff + _dot((hid * hid).astype(BF16), w2_ref[c0:c0 + FF_CHUNK, :])
    o_ref[...] = h1 + _rms(ff, gffpost_ref[...])


def _post_ffn(h, y_r, y_a, gates, wr, wa, wm, w1, w2, gpost, gpre, gffpost, layer):
    t = h.shape[0]
    tok = lambda cols: pl.BlockSpec((POST_TM, cols), lambda i: (i, 0))
    res = lambda shape: _layer_resident(shape, layer)
    return pl.pallas_call(
        _post_ffn_kernel,
        grid=(t // POST_TM,),
        in_specs=[tok(D_MODEL), tok(RET_V), tok(ATT_Q), tok(GATE_COLS),
                  res((RET_V, D_MODEL)), res((ATT_Q, D_MODEL)), res((D_MODEL, D_MODEL)),
                  res((D_MODEL, D_FF)), res((D_FF, D_MODEL)),
                  res((1, D_MODEL)), res((1, D_MODEL)), res((1, D_MODEL))],
        out_specs=tok(D_MODEL),
        out_shape=jax.ShapeDtypeStruct((t, D_MODEL), F32),
        compiler_params=pltpu.CompilerParams(dimension_semantics=("arbitrary",), vmem_limit_bytes=VMEM_LIMIT),
        name="post_ffn",
    )(h, y_r, y_a, gates, wr, wa, wm, w1, w2, gpost, gpre, gffpost)


def kernel(x, meta_tokens, w_in, w_ret_o, w_att_o, w_mix_o, w_ff1, w_ff2,
           norm_mix_pre, norm_mix_post, norm_ff_pre, norm_ff_post, ret_decay, attn_sink):
    batch, seq, d = x.shape
    depth = w_in.shape[0]
    ltok = seq + N_META
    assert d == D_MODEL and seq % (2 * CHUNK) == 0 and seq >= BAND and ltok % IN_TM == 0 and ltok % POST_TM == 0
    meta = jnp.broadcast_to(meta_tokens[None].astype(x.dtype), (batch, N_META, d))
    h = jnp.concatenate([x, meta], axis=1).reshape(batch * ltok, d)
    tables = _rotary_tables(seq)
    w_in_b, w_ret_b, w_att_b, w_mix_b, w_ff1_b, w_ff2_b = [
        w.astype(BF16) for w in (w_in, w_ret_o, w_att_o, w_mix_o, w_ff1, w_ff2)]
    gains = [g.astype(F32).reshape(depth, 1, d) for g in (norm_mix_pre, norm_mix_post, norm_ff_pre, norm_ff_post)]
    for l in range(depth):
        mix, gates = _in_proj(h, gains[0], w_in_b, tables, l, ltok // IN_TM)
        y_r = _retention(mix, _decay_lanes(ret_decay[l]), batch, ltok)
        y_a = _attention(mix, attn_sink[l].astype(F32), batch, ltok)
        h = _post_ffn(h, y_r, y_a, gates, w_ret_b, w_att_b, w_mix_b, w_ff1_b, w_ff2_b,
                      gains[1], gains[2], gains[3], l)
    return h.reshape(batch, ltok, d)[:, :seq]
```

```python
import functools

import jax
import jax.numpy as jnp
from jax import lax
from jax.experimental import pallas as pl
from jax.experimental.pallas import tpu as pltpu

F32 = jnp.float32
BF16 = jnp.bfloat16

D_MODEL = 1024
N_META = 16
CHUNK = 128
RET_HEADS = 8
RET_QK_DIM = 64
RET_V_DIM = 128
ATT_Q_HEADS = 8
ATT_KV_HEADS = 2
ATT_GROUP = ATT_Q_HEADS // ATT_KV_HEADS
ATT_HEAD_DIM = 64
ROPE_DIM = 16
ROPE_THETA = 500000.0
XPOS_THETA = 10000.0
D_FF = 4 * D_MODEL
EPS = 1e-6
NEG_INF = -1e30

RET_QK = RET_HEADS * RET_QK_DIM
RET_V = RET_HEADS * RET_V_DIM
ATT_Q = ATT_Q_HEADS * ATT_HEAD_DIM
ATT_KV = ATT_KV_HEADS * ATT_HEAD_DIM
OFF_QR = 0
OFF_KR = OFF_QR + RET_QK
OFF_VR = OFF_KR + RET_QK
OFF_GR = OFF_VR + RET_V
OFF_QA = OFF_GR + RET_V
OFF_KA = OFF_QA + ATT_Q
OFF_VA = OFF_KA + ATT_KV
MIX_COLS = OFF_VA + ATT_KV
GATE_COLS = 2 * D_MODEL
D_IN = MIX_COLS + GATE_COLS

LANES = 128
MXU_N = 256
VMEM_LIMIT = 56 * 1024 * 1024


def _dot(a, b):
    return jnp.dot(a, b, preferred_element_type=F32)


def _dot_nt(a, b):
    return lax.dot_general(a, b, (((1,), (1,)), ((), ())), preferred_element_type=F32)


def _rms(x, g):
    return x * lax.rsqrt(jnp.mean(x * x, axis=-1, keepdims=True) + EPS) * g


def _sigmoid(x):
    return 1.0 / (1.0 + jnp.exp(-x))


def _layer_resident(shape, layer):
    return pl.BlockSpec((None,) + shape, lambda *_: (layer,) + (0,) * len(shape), pipeline_mode=pl.Buffered(1))


def _rotary_tables(seq):
    pos = jnp.concatenate([jnp.arange(N_META, N_META + seq), jnp.arange(N_META)]).astype(F32)
    d = jnp.arange(LANES) % RET_QK_DIM

    def table(theta, rot_dim):
        half = rot_dim // 2
        freqs = jnp.power(jnp.float32(theta), -jnp.arange(0, rot_dim, 2, dtype=F32) / rot_dim)
        ang = pos[:, None] * freqs[None, :]
        cos, sin = jnp.cos(ang), jnp.sin(ang)
        idx = d % half
        rotated = (d < rot_dim)[None, :]
        cos_l = jnp.where(rotated, cos[:, idx], 1.0)
        sin_l = jnp.where(rotated, jnp.where((d < half)[None, :], -sin[:, idx], sin[:, idx]), 0.0)
        return cos_l, sin_l

    cr, sr = table(XPOS_THETA, RET_QK_DIM)
    ca, sa = table(ROPE_THETA, ROPE_DIM)
    qs_r = RET_QK_DIM ** -0.5
    qs_a = ATT_HEAD_DIM ** -0.5
    return (cr * qs_r, sr * qs_r, cr, sr, ca * qs_a, sa * qs_a, ca, sa)


def _rotate(x, cos, sin, half):
    lane = lax.broadcasted_iota(jnp.int32, x.shape, 1) % RET_QK_DIM
    partner = jnp.where(lane < half, pltpu.roll(x, LANES - half, 1), pltpu.roll(x, half, 1))
    return x * cos + partner * sin


IN_TM = 688


def _in_proj_kernel(h_ref, g_ref, w_ref, crq, srq, crk, srk, caq, saq, cak, sak, mix_ref, gate_ref):
    x = h_ref[...]
    u = _rms(x, g_ref[...]).astype(BF16)

    def proj(c0, n):
        return _dot(u, w_ref[:, c0:c0 + n])

    def rot_store(c0, cos_ref, sin_ref, half):
        p = proj(c0, MXU_N)
        for j in range(MXU_N // LANES):
            t = _rotate(p[:, j * LANES:(j + 1) * LANES], cos_ref[...], sin_ref[...], half)
            mix_ref[:, c0 + j * LANES:c0 + (j + 1) * LANES] = t.astype(BF16)

    for c0 in range(OFF_QR, OFF_KR, MXU_N):
        rot_store(c0, crq, srq, RET_QK_DIM // 2)
    for c0 in range(OFF_KR, OFF_VR, MXU_N):
        rot_store(c0, crk, srk, RET_QK_DIM // 2)
    for c0 in range(OFF_VR, OFF_GR, MXU_N):
        mix_ref[:, c0:c0 + MXU_N] = proj(c0, MXU_N).astype(BF16)
    for c0 in range(OFF_GR, OFF_QA, MXU_N):
        p = proj(c0, MXU_N)
        mix_ref[:, c0:c0 + MXU_N] = (p * _sigmoid(p)).astype(BF16)
    for c0 in range(OFF_QA, OFF_KA, MXU_N):
        rot_store(c0, caq, saq, ROPE_DIM // 2)
    p = proj(OFF_KA, MXU_N)
    mix_ref[:, OFF_KA:OFF_VA] = _rotate(p[:, :LANES], cak[...], sak[...], ROPE_DIM // 2).astype(BF16)
    mix_ref[:, OFF_VA:MIX_COLS] = p[:, LANES:].astype(BF16)
    for c0 in range(0, GATE_COLS, MXU_N):
        gate_ref[:, c0:c0 + MXU_N] = _sigmoid(proj(MIX_COLS + c0, MXU_N)).astype(BF16)


def _in_proj(h, gains, w, tables, layer, tiles_per_row):
    t = h.shape[0]
    tok = lambda cols: pl.BlockSpec((IN_TM, cols), lambda i: (i, 0))
    tab = pl.BlockSpec((IN_TM, LANES), lambda i: (i % tiles_per_row, 0))
    return pl.pallas_call(
        _in_proj_kernel,
        grid=(t // IN_TM,),
        in_specs=[tok(D_MODEL), _layer_resident((1, D_MODEL), layer), _layer_resident((D_MODEL, D_IN), layer)]
                 + [tab] * 8,
        out_specs=[tok(MIX_COLS), tok(GATE_COLS)],
        out_shape=[jax.ShapeDtypeStruct((t, MIX_COLS), BF16), jax.ShapeDtypeStruct((t, GATE_COLS), BF16)],
        compiler_params=pltpu.CompilerParams(dimension_semantics=("arbitrary",), vmem_limit_bytes=VMEM_LIMIT),
        name="in_proj",
    )(h, gains, w, *tables)


def _ret_kernel(rd_ref, q_ref, k_ref, v_ref, sg_ref, o_ref, tab_scr, dec_scr, pad_scr, padw_scr,
                kv_scr, st_scr, s_scr, *, n_chunks):
    C = CHUNK
    seq = n_chunks * C
    W = 2 * LANES
    H = RET_QK_DIM
    lg = -jnp.exp(rd_ref[...])
    lgf_a, lgf_b, lgb_a, lgb_b, lgf_p, lgb_p = [lg[i:i + 1, :] for i in range(6)]

    row = lax.broadcasted_iota(jnp.int32, (C, LANES), 0)
    col = lax.broadcasted_iota(jnp.int32, (C, LANES), 1)
    a = row.astype(F32)
    rel = (row - col).astype(F32)

    def decay_matrix(lf, lb):
        return jnp.where(rel >= 0, jnp.exp(lf * jnp.maximum(rel, 0.0)), jnp.exp(lb * jnp.maximum(-rel, 0.0)))

    tab_scr[0] = jnp.concatenate([jnp.exp(lgf_p * (a + 1.0)), jnp.exp(lgb_p * (C - a))], axis=1)
    tab_scr[1] = jnp.concatenate([jnp.exp(lgf_p * (C - 1.0 - a)), jnp.exp(lgb_p * a)], axis=1)
    tab_scr[2] = jnp.concatenate([decay_matrix(lgf_a, lgb_a), decay_matrix(lgf_b, lgb_b)], axis=1)
    top = row < H
    for i, (la, lb) in enumerate(((lgf_a, lgf_b), (lgb_a, lgb_b))):
        g = jnp.where(top, jnp.exp(la * C), jnp.exp(lb * C))
        dec_scr[i] = jnp.concatenate([g, g], axis=1)
    lo = col < H

    pad_rows = C - N_META
    for i, ref in enumerate((q_ref, k_ref)):
        pad_scr[i, 0:pad_rows, :] = jnp.zeros((pad_rows, LANES), BF16)
        pad_scr[i, pad_rows:C, :] = ref[seq:seq + N_META, :]
    for i, ref in enumerate((v_ref, sg_ref)):
        padw_scr[i, 0:pad_rows, :] = jnp.zeros((pad_rows, W), BF16)
        padw_scr[i, pad_rows:C, :] = ref[seq:seq + N_META, :]

    def seq_rows(c):
        return pl.ds(c * C, C)

    def kv(kc, vc):
        kf = kc.astype(F32)
        ks = jnp.concatenate([kf, kf], axis=1) * tab_scr[1]
        kt = jnp.concatenate([ks[:, :LANES].T, ks[:, LANES:].T], axis=0).astype(BF16)
        x = _dot(kt, vc)
        z = jnp.zeros((H, LANES), F32)
        rows = []
        for d in range(2):
            rows.append(jnp.concatenate([x[d * C:d * C + H, :LANES], z], axis=1))
            rows.append(jnp.concatenate([z, x[d * C + H:(d + 1) * C, LANES:]], axis=1))
        return jnp.concatenate(rows, axis=0)

    kv_scr[0] = kv(pad_scr[1], padw_scr[0])
    for c in range(n_chunks):
        r = seq_rows(c)
        kv_scr[c + 1] = kv(k_ref[r, :], v_ref[r, :])

    s_scr[...] = jnp.zeros((2, C, W), F32)
    for t in range(n_chunks + 1):
        u = n_chunks - t
        sf = s_scr[0]
        sb = s_scr[1]
        st_scr[t, 0:C, :] = sf.astype(BF16)
        st_scr[u, C:2 * C, :] = sb.astype(BF16)
        s_scr[0] = sf * dec_scr[0] + kv_scr[t, 0:C, :]
        s_scr[1] = sb * dec_scr[1] + kv_scr[u, C:2 * C, :]

    def emit(qc, kc, vc, sgc, st):
        qf = qc.astype(F32)
        kf = kc.astype(F32)
        k2 = jnp.concatenate([jnp.where(lo, kf, 0.0), jnp.where(lo, 0.0, kf)], axis=0).astype(BF16)
        p = (_dot_nt(qc, k2) * tab_scr[2]).astype(BF16)
        zb = jnp.zeros((C, LANES), BF16)
        vbd = jnp.concatenate([jnp.concatenate([vc[:, :LANES], zb], axis=1),
                               jnp.concatenate([zb, vc[:, LANES:]], axis=1)], axis=0)
        qs = (jnp.concatenate([qf, qf], axis=1) * tab_scr[0]).astype(BF16)
        o = _dot(p, vbd) + _dot(qs, st)
        halves = []
        for j in range(2):
            oh = o[:, j * LANES:(j + 1) * LANES]
            mu = jnp.mean(oh, axis=-1, keepdims=True)
            var = jnp.mean(jnp.square(oh - mu), axis=-1, keepdims=True)
            halves.append((oh - mu) * lax.rsqrt(var + EPS))
        return (jnp.concatenate(halves, axis=1) * sgc.astype(F32)).astype(BF16)

    for c in range(n_chunks):
        r = seq_rows(c)
        o_ref[r, :] = emit(q_ref[r, :], k_ref[r, :], v_ref[r, :], sg_ref[r, :], st_scr[c + 1])
    om = emit(pad_scr[0], pad_scr[1], padw_scr[0], padw_scr[1], st_scr[0])
    o_ref[seq:seq + N_META, :] = om[pad_rows:, :]


def _retention(mix, rd_lanes, batch, ltok):
    n_chunks = (ltok - N_META) // CHUNK
    pairs = RET_HEADS // 2
    blk = lambda cols, base: pl.BlockSpec((ltok, cols), lambda b, j: (b, base + j))
    return pl.pallas_call(
        functools.partial(_ret_kernel, n_chunks=n_chunks),
        grid=(batch, pairs),
        in_specs=[
            pl.BlockSpec((None, 8, LANES), lambda b, j: (j, 0, 0)),
            blk(LANES, OFF_QR // LANES),
            blk(LANES, OFF_KR // LANES),
            blk(2 * LANES, OFF_VR // (2 * LANES)),
            blk(2 * LANES, OFF_GR // (2 * LANES)),
        ],
        out_specs=pl.BlockSpec((ltok, 2 * LANES), lambda b, j: (b, j)),
        out_shape=jax.ShapeDtypeStruct((batch * ltok, RET_V), BF16),
        scratch_shapes=[
            pltpu.VMEM((3, CHUNK, 2 * LANES), F32),
            pltpu.VMEM((2, CHUNK, 2 * LANES), F32),
            pltpu.VMEM((2, CHUNK, LANES), BF16),
            pltpu.VMEM((2, CHUNK, 2 * LANES), BF16),
            pltpu.VMEM((n_chunks + 1, 2 * CHUNK, 2 * LANES), F32),
            pltpu.VMEM((n_chunks + 1, 2 * CHUNK, 2 * LANES), BF16),
            pltpu.VMEM((2, CHUNK, 2 * LANES), F32),
        ],
        compiler_params=pltpu.CompilerParams(dimension_semantics=("arbitrary", "arbitrary"),
                                             vmem_limit_bytes=VMEM_LIMIT),
        name="retention",
    )(rd_lanes, mix, mix, mix, mix)


def _decay_lanes(ret_decay_l):
    rd = ret_decay_l.astype(F32)
    pairs = RET_HEADS // 2
    fa, fb = rd[0, 0::2], rd[0, 1::2]
    ba, bb = rd[1, 0::2], rd[1, 1::2]
    full = lambda v: jnp.broadcast_to(v[:, None], (pairs, LANES))
    split = lambda x, y: jnp.concatenate([jnp.broadcast_to(x[:, None], (pairs, RET_QK_DIM)),
                                          jnp.broadcast_to(y[:, None], (pairs, RET_QK_DIM))], axis=1)
    rows = [full(fa), full(fb), full(ba), full(bb), split(fa, fb), split(ba, bb), full(fa), full(fa)]
    return jnp.stack(rows, axis=1)


BAND = 3 * CHUNK
KEYS = BAND + CHUNK


MASK_UPPER, MASK_LOWER, MASK_META, MASK_METAQ, MASK_NONE = range(5)
MASK_ALL = None
N_MASKS = 5
SOFTMAX_ROWS = 32


def _att_kernel(sink_ref, q_ref, k_ref, v_ref, o_ref, k_scr, v_scr, bias_scr, sa_scr, sb_scr, e_scr, t_scr,
                *, n_blocks):
    C = CHUNK
    seq = n_blocks * C
    g = pl.program_id(1)
    ltok = seq + N_META
    lane = lax.broadcasted_iota(jnp.int32, (ltok, LANES), 1)
    lo = lane < ATT_HEAD_DIM
    first = g == 0

    def own_head_low(ref):
        x = ref[...].astype(F32)
        return jnp.where(first, x, pltpu.roll(x, ATT_HEAD_DIM, 1))

    kk = own_head_low(k_ref)
    kdup = jnp.where(lo, kk, pltpu.roll(kk, ATT_HEAD_DIM, 1)).astype(BF16)
    vone = jnp.where(lo, own_head_low(v_ref), 1.0).astype(BF16)
    pad = jnp.zeros((C - N_META, LANES), BF16)
    for scr, val in ((k_scr, kdup), (v_scr, vone)):
        meta_tile = jnp.concatenate([val[seq:ltok, :], pad], axis=0)
        for j in range(n_blocks):
            t = j + j // 2
            scr[t * C:(t + 1) * C, :] = val[j * C:(j + 1) * C, :]
            if j % 2 == 1:
                scr[(t + 1) * C:(t + 2) * C, :] = meta_tile

    row = lax.broadcasted_iota(jnp.int32, (C, LANES), 0)
    col = lax.broadcasted_iota(jnp.int32, (C, LANES), 1)
    visible = lambda cond: jnp.where(cond, 0.0, NEG_INF)
    bias_scr[MASK_UPPER] = visible(col >= row)
    bias_scr[MASK_LOWER] = visible(col <= row)
    bias_scr[MASK_META] = visible(col < N_META)
    bias_scr[MASK_METAQ] = visible(col <= row + (C - N_META))
    bias_scr[MASK_NONE] = visible(col < 0)

    def scores(q4, slab_start, s_scr, slot):
        R = q4.shape[0]
        qf = q4.astype(F32)
        lo_r = lax.broadcasted_iota(jnp.int32, (R, LANES), 1) < ATT_HEAD_DIM
        stacked = []
        for p in range(ATT_GROUP // 2):
            qp = qf[:, p * LANES:(p + 1) * LANES]
            stacked += [jnp.where(lo_r, qp, 0.0), jnp.where(lo_r, 0.0, qp)]
        qs = jnp.concatenate(stacked, axis=0).astype(BF16)
        s_scr[slot, 0:ATT_GROUP * R, :] = _dot_nt(qs, k_scr[pl.ds(slab_start, KEYS), :])

    def finish(R, slab_start, masks, s_scr, slot):
        eslot = slot if s_scr is sa_scr else 2 + slot
        SR = min(R, SOFTMAX_ROWS)
        lo_r = lax.broadcasted_iota(jnp.int32, (R, LANES), 1) < ATT_HEAD_DIM
        slab = pl.ds(slab_start, KEYS)
        for h in range(ATT_GROUP):
            sink = sink_ref[g * ATT_GROUP + h]
            for i in range(R // SR):
                rows = pl.ds(h * R + i * SR, SR)
                tiles = []
                for t, mask in enumerate(masks):
                    x = s_scr[slot, rows, t * LANES:(t + 1) * LANES]
                    if mask is not MASK_ALL:
                        x = x + bias_scr[mask, i * SR:(i + 1) * SR, :]
                    tiles.append(x)
                top = jnp.maximum(jnp.maximum(tiles[0], tiles[1]), jnp.maximum(tiles[2], tiles[3]))
                m = jnp.maximum(jnp.max(top, axis=-1, keepdims=True), sink)
                for t in range(len(masks)):
                    e_scr[eslot, rows, t * LANES:(t + 1) * LANES] = jnp.exp(tiles[t] - m).astype(BF16)
                t_scr[eslot, rows, :] = jnp.broadcast_to(jnp.exp(sink - m), (SR, LANES))
        res = _dot(e_scr[eslot, 0:ATT_GROUP * R, :], v_scr[slab, :])
        outs = []
        for p in range(ATT_GROUP // 2):
            ev, od = pl.ds(2 * p * R, R), pl.ds((2 * p + 1) * R, R)
            r_ev, r_od = res[2 * p * R:(2 * p + 1) * R, :], res[(2 * p + 1) * R:(2 * p + 2) * R, :]
            num = jnp.where(lo_r, r_ev, pltpu.roll(r_od, ATT_HEAD_DIM, 1))
            den = (jnp.where(lo_r, pltpu.roll(r_ev, ATT_HEAD_DIM, 1), r_od)
                   + jnp.where(lo_r, t_scr[eslot, ev, :], t_scr[eslot, od, :]))
            outs.append(num / den)
        return jnp.concatenate(outs, axis=1).astype(BF16)

    odd_masks = (MASK_UPPER, MASK_ALL, MASK_META, MASK_LOWER)
    even_masks = (MASK_UPPER, MASK_META, MASK_ALL, MASK_LOWER)
    n_pairs = n_blocks // 2 - 1

    def q_rows(n):
        return pl.ds(pl.multiple_of(n * C, C), C)

    def pair_scores(i, s_scr):
        scores(q_ref[q_rows(2 * i + 1), :], pl.multiple_of(3 * i * C, C), s_scr, 0)
        scores(q_ref[q_rows(2 * i + 2), :], pl.multiple_of((3 * i + 1) * C, C), s_scr, 1)

    def pair_finish(i, s_scr):
        o_ref[q_rows(2 * i + 1), :] = finish(C, pl.multiple_of(3 * i * C, C), odd_masks, s_scr, 0)
        o_ref[q_rows(2 * i + 2), :] = finish(C, pl.multiple_of((3 * i + 1) * C, C), even_masks, s_scr, 1)

    pair_scores(0, sa_scr)
    scores(q_ref[0:C, :], 0, sb_scr, 0)
    o_ref[0:C, :] = finish(C, 0, (MASK_ALL, MASK_LOWER, MASK_META, MASK_NONE), sb_scr, 0)

    def body(j, carry):
        pair_scores(2 * j + 1, sb_scr)
        pair_finish(2 * j, sa_scr)
        pair_scores(2 * j + 2, sa_scr)
        pair_finish(2 * j + 1, sb_scr)
        return carry

    lax.fori_loop(0, (n_pairs - 1) // 2, body, 0)
    last = n_blocks - 1
    last_slab = ((last - 2) + (last - 2) // 2) * C
    scores(q_ref[last * C:seq, :], last_slab, sb_scr, 0)
    scores(q_ref[seq:ltok, :], 0, sb_scr, 1)
    pair_finish(n_pairs - 1, sa_scr)
    o_ref[last * C:seq, :] = finish(C, last_slab, (MASK_NONE, MASK_META, MASK_UPPER, MASK_ALL), sb_scr, 0)
    o_ref[seq:ltok, :] = finish(N_META, 0, (MASK_METAQ, MASK_NONE, MASK_META, MASK_NONE), sb_scr, 1)


def _attention(mix, sink, batch, ltok):
    n_blocks = (ltok - N_META) // CHUNK
    assert n_blocks % 4 == 0
    slab_rows = (n_blocks + n_blocks // 2) * CHUNK
    stacked = ATT_GROUP * CHUNK
    return pl.pallas_call(
        functools.partial(_att_kernel, n_blocks=n_blocks),
        grid=(batch, ATT_KV_HEADS),
        in_specs=[
            pl.BlockSpec(memory_space=pltpu.SMEM),
            pl.BlockSpec((ltok, 2 * LANES), lambda b, g: (b, OFF_QA // (2 * LANES) + g)),
            pl.BlockSpec((ltok, LANES), lambda b, g: (b, OFF_KA // LANES)),
            pl.BlockSpec((ltok, LANES), lambda b, g: (b, OFF_VA // LANES)),
        ],
        out_specs=pl.BlockSpec((ltok, 2 * LANES), lambda b, g: (b, g)),
        out_shape=jax.ShapeDtypeStruct((batch * ltok, ATT_Q), BF16),
        scratch_shapes=[
            pltpu.VMEM((slab_rows, LANES), BF16),
            pltpu.VMEM((slab_rows, LANES), BF16),
            pltpu.VMEM((N_MASKS, CHUNK, LANES), F32),
            pltpu.VMEM((2, stacked, KEYS), F32),
            pltpu.VMEM((2, stacked, KEYS), F32),
            pltpu.VMEM((4, stacked, KEYS), BF16),
            pltpu.VMEM((4, stacked, LANES), F32),
        ],
        compiler_params=pltpu.CompilerParams(dimension_semantics=("arbitrary", "arbitrary"),
                                             vmem_limit_bytes=VMEM_LIMIT),
        name="attention",
    )(sink, mix, mix, mix)


POST_TM = 344
FF_CHUNK = 1024


def _post_ffn_kernel(h_ref, yr_ref, ya_ref, gt_ref, wr_ref, wa_ref, wm_ref, w1_ref, w2_ref,
                     gpost_ref, gpre_ref, gffpost_ref, o_ref):
    y_r = _dot(yr_ref[...], wr_ref[...])
    y_a = _dot(ya_ref[...], wa_ref[...])
    z = gt_ref[:, :D_MODEL].astype(F32) * y_r + gt_ref[:, D_MODEL:].astype(F32) * y_a
    mix = _dot(z.astype(BF16), wm_ref[...])
    h1 = h_ref[...] + _rms(mix, gpost_ref[...])
    u = _rms(h1, gpre_ref[...]).astype(BF16)
    ff = jnp.zeros(h1.shape, F32)
    for c0 in range(0, D_FF, FF_CHUNK):
        hid = jnp.maximum(_dot(u, w1_ref[:, c0:c0 + FF_CHUNK]), 0.0)
        ff = ff + _dot((hid * hid).astype(BF16), w2_ref[c0:c0 + FF_CHUNK, :])
    o_ref[...] = h1 + _rms(ff, gffpost_ref[...])


def _post_ffn(h, y_r, y_a, gates, wr, wa, wm, w1, w2, gpost, gpre, gffpost, layer):
    t = h.shape[0]
    tok = lambda cols: pl.BlockSpec((POST_TM, cols), lambda i: (i, 0))
    res = lambda shape: _layer_resident(shape, layer)
    return pl.pallas_call(
        _post_ffn_kernel,
        grid=(t // POST_TM,),
        in_specs=[tok(D_MODEL), tok(RET_V), tok(ATT_Q), tok(GATE_COLS),
                  res((RET_V, D_MODEL)), res((ATT_Q, D_MODEL)), res((D_MODEL, D_MODEL)),
                  res((D_MODEL, D_FF)), res((D_FF, D_MODEL)),
                  res((1, D_MODEL)), res((1, D_MODEL)), res((1, D_MODEL))],
        out_specs=tok(D_MODEL),
        out_shape=jax.ShapeDtypeStruct((t, D_MODEL), F32),
        compiler_params=pltpu.CompilerParams(dimension_semantics=("arbitrary",), vmem_limit_bytes=VMEM_LIMIT),
        name="post_ffn",
    )(h, y_r, y_a, gates, wr, wa, wm, w1, w2, gpost, gpre, gffpost)


def kernel(x, meta_tokens, w_in, w_ret_o, w_att_o, w_mix_o, w_ff1, w_ff2,
           norm_mix_pre, norm_mix_post, norm_ff_pre, norm_ff_post, ret_decay, attn_sink):
    batch, seq, d = x.shape
    depth = w_in.shape[0]
    ltok = seq + N_META
    assert d == D_MODEL and seq % (2 * CHUNK) == 0 and seq >= BAND and ltok % IN_TM == 0 and ltok % POST_TM == 0
    meta = jnp.broadcast_to(meta_tokens[None].astype(x.dtype), (batch, N_META, d))
    h = jnp.concatenate([x, meta], axis=1).reshape(batch * ltok, d)
    tables = _rotary_tables(seq)
    w_in_b, w_ret_b, w_att_b, w_mix_b, w_ff1_b, w_ff2_b = [
        w.astype(BF16) for w in (w_in, w_ret_o, w_att_o, w_mix_o, w_ff1, w_ff2)]
    gains = [g.astype(F32).reshape(depth, 1, d) for g in (norm_mix_pre, norm_mix_post, norm_ff_pre, norm_ff_post)]
    for l in range(depth):
        mix, gates = _in_proj(h, gains[0], w_in_b, tables, l, ltok // IN_TM)
        y_r = _retention(mix, _decay_lanes(ret_decay[l]), batch, ltok)
        y_a = _attention(mix, attn_sink[l].astype(F32), batch, ltok)
        h = _post_ffn(h, y_r, y_a, gates, w_ret_b, w_att_b, w_mix_b, w_ff1_b, w_ff2_b,
                      gains[1], gains[2], gains[3], l)
    return h.reshape(batch, ltok, d)[:, :seq]
```

```python
import functools

import jax
import jax.numpy as jnp
from jax import lax
from jax.experimental import pallas as pl
from jax.experimental.pallas import tpu as pltpu

F32 = jnp.float32
BF16 = jnp.bfloat16

D_MODEL = 1024
N_META = 16
CHUNK = 128
RET_HEADS = 8
RET_QK_DIM = 64
RET_V_DIM = 128
ATT_Q_HEADS = 8
ATT_KV_HEADS = 2
ATT_GROUP = ATT_Q_HEADS // ATT_KV_HEADS
ATT_HEAD_DIM = 64
ROPE_DIM = 16
ROPE_THETA = 500000.0
XPOS_THETA = 10000.0
D_FF = 4 * D_MODEL
EPS = 1e-6
NEG_INF = -1e30

RET_QK = RET_HEADS * RET_QK_DIM
RET_V = RET_HEADS * RET_V_DIM
ATT_Q = ATT_Q_HEADS * ATT_HEAD_DIM
ATT_KV = ATT_KV_HEADS * ATT_HEAD_DIM
OFF_QR = 0
OFF_KR = OFF_QR + RET_QK
OFF_VR = OFF_KR + RET_QK
OFF_GR = OFF_VR + RET_V
OFF_QA = OFF_GR + RET_V
OFF_KA = OFF_QA + ATT_Q
OFF_VA = OFF_KA + ATT_KV
MIX_COLS = OFF_VA + ATT_KV
GATE_COLS = 2 * D_MODEL
D_IN = MIX_COLS + GATE_COLS

LANES = 128
MXU_N = 256
VMEM_LIMIT = 60 * 1024 * 1024


def _dot(a, b):
    return jnp.dot(a, b, preferred_element_type=F32)


def _dot_nt(a, b):
    return lax.dot_general(a, b, (((1,), (1,)), ((), ())), preferred_element_type=F32)


def _rms(x, g):
    return x * lax.rsqrt(jnp.mean(x * x, axis=-1, keepdims=True) + EPS) * g


def _sigmoid(x):
    return 1.0 / (1.0 + jnp.exp(-x))


def _layer_resident(shape, layer):
    return pl.BlockSpec((None,) + shape, lambda *_: (layer,) + (0,) * len(shape), pipeline_mode=pl.Buffered(1))


def _rotary_tables(seq):
    pos = jnp.concatenate([jnp.arange(N_META, N_META + seq), jnp.arange(N_META)]).astype(F32)
    d = jnp.arange(LANES) % RET_QK_DIM

    def table(theta, rot_dim):
        half = rot_dim // 2
        freqs = jnp.power(jnp.float32(theta), -jnp.arange(0, rot_dim, 2, dtype=F32) / rot_dim)
        ang = pos[:, None] * freqs[None, :]
        cos, sin = jnp.cos(ang), jnp.sin(ang)
        idx = d % half
        rotated = (d < rot_dim)[None, :]
        cos_l = jnp.where(rotated, cos[:, idx], 1.0)
        sin_l = jnp.where(rotated, jnp.where((d < half)[None, :], -sin[:, idx], sin[:, idx]), 0.0)
        return cos_l, sin_l

    cr, sr = table(XPOS_THETA, RET_QK_DIM)
    ca, sa = table(ROPE_THETA, ROPE_DIM)
    qs_r = RET_QK_DIM ** -0.5
    qs_a = ATT_HEAD_DIM ** -0.5
    return (cr * qs_r, sr * qs_r, cr, sr, ca * qs_a, sa * qs_a, ca, sa)


def _rotate(x, cos, sin, half):
    lane = lax.broadcasted_iota(jnp.int32, x.shape, 1) % RET_QK_DIM
    partner = jnp.where(lane < half, pltpu.roll(x, LANES - half, 1), pltpu.roll(x, half, 1))
    return x * cos + partner * sin


IN_TM = 688
IN_SUB = ((0, 352), (352, 336))


def _in_proj_kernel(h_ref, g_ref, w_ref, crq, srq, crk, srk, caq, saq, cak, sak, mix_ref, gate_ref):
    def chain(r):
        u = _rms(h_ref[r, :], g_ref[...]).astype(BF16)
        yield

        def proj(c0, n):
            return _dot(u, w_ref[:, c0:c0 + n])

        def rot_store(c0, cos_ref, sin_ref, half):
            p = proj(c0, MXU_N)
            for j in range(MXU_N // LANES):
                t = _rotate(p[:, j * LANES:(j + 1) * LANES], cos_ref[r, :], sin_ref[r, :], half)
                mix_ref[r, c0 + j * LANES:c0 + (j + 1) * LANES] = t.astype(BF16)

        for c0 in range(OFF_QR, OFF_KR, MXU_N):
            rot_store(c0, crq, srq, RET_QK_DIM // 2)
            yield
        for c0 in range(OFF_KR, OFF_VR, MXU_N):
            rot_store(c0, crk, srk, RET_QK_DIM // 2)
            yield
        for c0 in range(OFF_VR, OFF_GR, MXU_N):
            mix_ref[r, c0:c0 + MXU_N] = proj(c0, MXU_N).astype(BF16)
            yield
        for c0 in range(OFF_GR, OFF_QA, MXU_N):
            p = proj(c0, MXU_N)
            mix_ref[r, c0:c0 + MXU_N] = (p * _sigmoid(p)).astype(BF16)
            yield
        for c0 in range(OFF_QA, OFF_KA, MXU_N):
            rot_store(c0, caq, saq, ROPE_DIM // 2)
            yield
        p = proj(OFF_KA, MXU_N)
        mix_ref[r, OFF_KA:OFF_VA] = _rotate(p[:, :LANES], cak[r, :], sak[r, :], ROPE_DIM // 2).astype(BF16)
        mix_ref[r, OFF_VA:MIX_COLS] = p[:, LANES:].astype(BF16)
        yield
        for c0 in range(0, GATE_COLS, MXU_N):
            gate_ref[r, c0:c0 + MXU_N] = _sigmoid(proj(MIX_COLS + c0, MXU_N)).astype(BF16)
            yield

    chains = [chain(pl.ds(start, rows)) for start, rows in IN_SUB]
    while chains:
        chains = [c for c in chains if next(c, "done") != "done"]


def _in_proj(h, gains, w, tables, layer, tiles_per_row):
    t = h.shape[0]
    tok = lambda cols: pl.BlockSpec((IN_TM, cols), lambda i: (i, 0))
    tab = pl.BlockSpec((IN_TM, LANES), lambda i: (i % tiles_per_row, 0))
    return pl.pallas_call(
        _in_proj_kernel,
        grid=(t // IN_TM,),
        in_specs=[tok(D_MODEL), _layer_resident((1, D_MODEL), layer), _layer_resident((D_MODEL, D_IN), layer)]
                 + [tab] * 8,
        out_specs=[tok(MIX_COLS), tok(GATE_COLS)],
        out_shape=[jax.ShapeDtypeStruct((t, MIX_COLS), BF16), jax.ShapeDtypeStruct((t, GATE_COLS), BF16)],
        compiler_params=pltpu.CompilerParams(dimension_semantics=("arbitrary",), vmem_limit_bytes=VMEM_LIMIT),
        name="in_proj",
    )(h, gains, w, *tables)


def _ret_kernel(rd_ref, q_ref, k_ref, v_ref, sg_ref, o_ref, tab_scr, dec_scr, pad_scr, padw_scr,
                kv_scr, st_scr, s_scr, *, n_chunks):
    C = CHUNK
    seq = n_chunks * C
    W = 2 * LANES
    H = RET_QK_DIM
    lg = -jnp.exp(rd_ref[...])
    lgf_a, lgf_b, lgb_a, lgb_b, lgf_p, lgb_p = [lg[i:i + 1, :] for i in range(6)]

    row = lax.broadcasted_iota(jnp.int32, (C, LANES), 0)
    col = lax.broadcasted_iota(jnp.int32, (C, LANES), 1)
    a = row.astype(F32)
    rel = (row - col).astype(F32)

    def decay_matrix(lf, lb):
        return jnp.where(rel >= 0, jnp.exp(lf * jnp.maximum(rel, 0.0)), jnp.exp(lb * jnp.maximum(-rel, 0.0)))

    tab_scr[0] = jnp.concatenate([jnp.exp(lgf_p * (a + 1.0)), jnp.exp(lgb_p * (C - a))], axis=1)
    tab_scr[1] = jnp.concatenate([jnp.exp(lgf_p * (C - 1.0 - a)), jnp.exp(lgb_p * a)], axis=1)
    tab_scr[2] = jnp.concatenate([decay_matrix(lgf_a, lgb_a), decay_matrix(lgf_b, lgb_b)], axis=1)
    top = row < H
    for i, (la, lb) in enumerate(((lgf_a, lgf_b), (lgb_a, lgb_b))):
        g = jnp.where(top, jnp.exp(la * C), jnp.exp(lb * C))
        dec_scr[i] = jnp.concatenate([g, g], axis=1)
    lo = col < H

    pad_rows = C - N_META
    for i, ref in enumerate((q_ref, k_ref)):
        pad_scr[i, 0:pad_rows, :] = jnp.zeros((pad_rows, LANES), BF16)
        pad_scr[i, pad_rows:C, :] = ref[seq:seq + N_META, :]
    for i, ref in enumerate((v_ref, sg_ref)):
        padw_scr[i, 0:pad_rows, :] = jnp.zeros((pad_rows, W), BF16)
        padw_scr[i, pad_rows:C, :] = ref[seq:seq + N_META, :]

    def seq_rows(c):
        return pl.ds(c * C, C)

    def kv(kc, vc):
        kf = kc.astype(F32)
        ks = jnp.concatenate([kf, kf], axis=1) * tab_scr[1]
        kt = jnp.concatenate([ks[:, :LANES].T, ks[:, LANES:].T], axis=0).astype(BF16)
        x = _dot(kt, vc)
        z = jnp.zeros((H, LANES), F32)
        rows = []
        for d in range(2):
            rows.append(jnp.concatenate([x[d * C:d * C + H, :LANES], z], axis=1))
            rows.append(jnp.concatenate([z, x[d * C + H:(d + 1) * C, LANES:]], axis=1))
        return jnp.concatenate(rows, axis=0)

    kv_scr[0] = kv(pad_scr[1], padw_scr[0])
    for c in range(n_chunks):
        r = seq_rows(c)
        kv_scr[c + 1] = kv(k_ref[r, :], v_ref[r, :])

    s_scr[...] = jnp.zeros((2, C, W), F32)
    for t in range(n_chunks + 1):
        u = n_chunks - t
        sf = s_scr[0]
        sb = s_scr[1]
        st_scr[t, 0:C, :] = sf.astype(BF16)
        st_scr[u, C:2 * C, :] = sb.astype(BF16)
        s_scr[0] = sf * dec_scr[0] + kv_scr[t, 0:C, :]
        s_scr[1] = sb * dec_scr[1] + kv_scr[u, C:2 * C, :]

    def emit(qc, kc, vc, sgc, st):
        qf = qc.astype(F32)
        kf = kc.astype(F32)
        k2 = jnp.concatenate([jnp.where(lo, kf, 0.0), jnp.where(lo, 0.0, kf)], axis=0).astype(BF16)
        p = (_dot_nt(qc, k2) * tab_scr[2]).astype(BF16)
        zb = jnp.zeros((C, LANES), BF16)
        vbd = jnp.concatenate([jnp.concatenate([vc[:, :LANES], zb], axis=1),
                               jnp.concatenate([zb, vc[:, LANES:]], axis=1)], axis=0)
        qs = (jnp.concatenate([qf, qf], axis=1) * tab_scr[0]).astype(BF16)
        o = _dot(p, vbd) + _dot(qs, st)
        halves = []
        for j in range(2):
            oh = o[:, j * LANES:(j + 1) * LANES]
            mu = jnp.mean(oh, axis=-1, keepdims=True)
            var = jnp.mean(jnp.square(oh - mu), axis=-1, keepdims=True)
            halves.append((oh - mu) * lax.rsqrt(var + EPS))
        return (jnp.concatenate(halves, axis=1) * sgc.astype(F32)).astype(BF16)

    for c in range(n_chunks):
        r = seq_rows(c)
        o_ref[r, :] = emit(q_ref[r, :], k_ref[r, :], v_ref[r, :], sg_ref[r, :], st_scr[c + 1])
    om = emit(pad_scr[0], pad_scr[1], padw_scr[0], padw_scr[1], st_scr[0])
    o_ref[seq:seq + N_META, :] = om[pad_rows:, :]


def _retention(mix, rd_lanes, batch, ltok):
    n_chunks = (ltok - N_META) // CHUNK
    pairs = RET_HEADS // 2
    blk = lambda cols, base: pl.BlockSpec((ltok, cols), lambda b, j: (b, base + j))
    return pl.pallas_call(
        functools.partial(_ret_kernel, n_chunks=n_chunks),
        grid=(batch, pairs),
        in_specs=[
            pl.BlockSpec((None, 8, LANES), lambda b, j: (j, 0, 0)),
            blk(LANES, OFF_QR // LANES),
            blk(LANES, OFF_KR // LANES),
            blk(2 * LANES, OFF_VR // (2 * LANES)),
            blk(2 * LANES, OFF_GR // (2 * LANES)),
        ],
        out_specs=pl.BlockSpec((ltok, 2 * LANES), lambda b, j: (b, j)),
        out_shape=jax.ShapeDtypeStruct((batch * ltok, RET_V), BF16),
        scratch_shapes=[
            pltpu.VMEM((3, CHUNK, 2 * LANES), F32),
            pltpu.VMEM((2, CHUNK, 2 * LANES), F32),
            pltpu.VMEM((2, CHUNK, LANES), BF16),
            pltpu.VMEM((2, CHUNK, 2 * LANES), BF16),
            pltpu.VMEM((n_chunks + 1, 2 * CHUNK, 2 * LANES), F32),
            pltpu.VMEM((n_chunks + 1, 2 * CHUNK, 2 * LANES), BF16),
            pltpu.VMEM((2, CHUNK, 2 * LANES), F32),
        ],
        compiler_params=pltpu.CompilerParams(dimension_semantics=("arbitrary", "arbitrary"),
                                             vmem_limit_bytes=VMEM_LIMIT),
        name="retention",
    )(rd_lanes, mix, mix, mix, mix)


def _decay_lanes(ret_decay_l):
    rd = ret_decay_l.astype(F32)
    pairs = RET_HEADS // 2
    fa, fb = rd[0, 0::2], rd[0, 1::2]
    ba, bb = rd[1, 0::2], rd[1, 1::2]
    full = lambda v: jnp.broadcast_to(v[:, None], (pairs, LANES))
    split = lambda x, y: jnp.concatenate([jnp.broadcast_to(x[:, None], (pairs, RET_QK_DIM)),
                                          jnp.broadcast_to(y[:, None], (pairs, RET_QK_DIM))], axis=1)
    rows = [full(fa), full(fb), full(ba), full(bb), split(fa, fb), split(ba, bb), full(fa), full(fa)]
    return jnp.stack(rows, axis=1)


BAND = 3 * CHUNK
KEYS = BAND + CHUNK


MASK_UPPER, MASK_LOWER, MASK_META, MASK_METAQ, MASK_NONE = range(5)
MASK_ALL = None
N_MASKS = 5
SOFTMAX_ROWS = 32


def _att_kernel(sink_ref, q_ref, k_ref, v_ref, o_ref, k_scr, v_scr, bias_scr, sa_scr, sb_scr, e_scr, t_scr,
                *, n_blocks):
    C = CHUNK
    seq = n_blocks * C
    g = pl.program_id(1)
    ltok = seq + N_META
    lane = lax.broadcasted_iota(jnp.int32, (ltok, LANES), 1)
    lo = lane < ATT_HEAD_DIM
    first = g == 0

    def own_head_low(ref):
        x = ref[...].astype(F32)
        return jnp.where(first, x, pltpu.roll(x, ATT_HEAD_DIM, 1))

    kk = own_head_low(k_ref)
    kdup = jnp.where(lo, kk, pltpu.roll(kk, ATT_HEAD_DIM, 1)).astype(BF16)
    vone = jnp.where(lo, own_head_low(v_ref), 1.0).astype(BF16)
    pad = jnp.zeros((C - N_META, LANES), BF16)
    for scr, val in ((k_scr, kdup), (v_scr, vone)):
        meta_tile = jnp.concatenate([val[seq:ltok, :], pad], axis=0)
        for j in range(n_blocks):
            t = j + j // 2
            scr[t * C:(t + 1) * C, :] = val[j * C:(j + 1) * C, :]
            if j % 2 == 1:
                scr[(t + 1) * C:(t + 2) * C, :] = meta_tile

    row = lax.broadcasted_iota(jnp.int32, (C, LANES), 0)
    col = lax.broadcasted_iota(jnp.int32, (C, LANES), 1)
    visible = lambda cond: jnp.where(cond, 0.0, NEG_INF)
    bias_scr[MASK_UPPER] = visible(col >= row)
    bias_scr[MASK_LOWER] = visible(col <= row)
    bias_scr[MASK_META] = visible(col < N_META)
    bias_scr[MASK_METAQ] = visible(col <= row + (C - N_META))
    bias_scr[MASK_NONE] = visible(col < 0)

    def scores(q4, slab_start, s_scr, slot):
        R = q4.shape[0]
        qf = q4.astype(F32)
        lo_r = lax.broadcasted_iota(jnp.int32, (R, LANES), 1) < ATT_HEAD_DIM
        stacked = []
        for p in range(ATT_GROUP // 2):
            qp = qf[:, p * LANES:(p + 1) * LANES]
            stacked += [jnp.where(lo_r, qp, 0.0), jnp.where(lo_r, 0.0, qp)]
        qs = jnp.concatenate(stacked, axis=0).astype(BF16)
        s_scr[slot, 0:ATT_GROUP * R, :] = _dot_nt(qs, k_scr[pl.ds(slab_start, KEYS), :])

    def finish(R, slab_start, masks, s_scr, slot):
        eslot = slot if s_scr is sa_scr else 2 + slot
        SR = min(R, SOFTMAX_ROWS)
        lo_r = lax.broadcasted_iota(jnp.int32, (R, LANES), 1) < ATT_HEAD_DIM
        slab = pl.ds(slab_start, KEYS)
        for h in range(ATT_GROUP):
            sink = sink_ref[g * ATT_GROUP + h]
            for i in range(R // SR):
                rows = pl.ds(h * R + i * SR, SR)
                tiles = []
                for t, mask in enumerate(masks):
                    x = s_scr[slot, rows, t * LANES:(t + 1) * LANES]
                    if mask is not MASK_ALL:
                        x = x + bias_scr[mask, i * SR:(i + 1) * SR, :]
                    tiles.append(x)
                top = jnp.maximum(jnp.maximum(tiles[0], tiles[1]), jnp.maximum(tiles[2], tiles[3]))
                m = jnp.maximum(jnp.max(top, axis=-1, keepdims=True), sink)
                for t in range(len(masks)):
                    e_scr[eslot, rows, t * LANES:(t + 1) * LANES] = jnp.exp(tiles[t] - m).astype(BF16)
                t_scr[eslot, rows, :] = jnp.broadcast_to(jnp.exp(sink - m), (SR, LANES))
        res = _dot(e_scr[eslot, 0:ATT_GROUP * R, :], v_scr[slab, :])
        outs = []
        for p in range(ATT_GROUP // 2):
            ev, od = pl.ds(2 * p * R, R), pl.ds((2 * p + 1) * R, R)
            r_ev, r_od = res[2 * p * R:(2 * p + 1) * R, :], res[(2 * p + 1) * R:(2 * p + 2) * R, :]
            num = jnp.where(lo_r, r_ev, pltpu.roll(r_od, ATT_HEAD_DIM, 1))
            den = (jnp.where(lo_r, pltpu.roll(r_ev, ATT_HEAD_DIM, 1), r_od)
                   + jnp.where(lo_r, t_scr[eslot, ev, :], t_scr[eslot, od, :]))
            outs.append(num / den)
        return jnp.concatenate(outs, axis=1).astype(BF16)

    odd_masks = (MASK_UPPER, MASK_ALL, MASK_META, MASK_LOWER)
    even_masks = (MASK_UPPER, MASK_META, MASK_ALL, MASK_LOWER)
    n_pairs = n_blocks // 2 - 1

    def q_rows(n):
        return pl.ds(pl.multiple_of(n * C, C), C)

    def pair_scores(i, s_scr):
        scores(q_ref[q_rows(2 * i + 1), :], pl.multiple_of(3 * i * C, C), s_scr, 0)
        scores(q_ref[q_rows(2 * i + 2), :], pl.multiple_of((3 * i + 1) * C, C), s_scr, 1)

    def pair_finish(i, s_scr):
        o_ref[q_rows(2 * i + 1), :] = finish(C, pl.multiple_of(3 * i * C, C), odd_masks, s_scr, 0)
        o_ref[q_rows(2 * i + 2), :] = finish(C, pl.multiple_of((3 * i + 1) * C, C), even_masks, s_scr, 1)

    pair_scores(0, sa_scr)
    scores(q_ref[0:C, :], 0, sb_scr, 0)
    o_ref[0:C, :] = finish(C, 0, (MASK_ALL, MASK_LOWER, MASK_META, MASK_NONE), sb_scr, 0)

    def body(j, carry):
        pair_scores(2 * j + 1, sb_scr)
        pair_finish(2 * j, sa_scr)
        pair_scores(2 * j + 2, sa_scr)
        pair_finish(2 * j + 1, sb_scr)
        return carry

    lax.fori_loop(0, (n_pairs - 1) // 2, body, 0)
    last = n_blocks - 1
    last_slab = ((last - 2) + (last - 2) // 2) * C
    scores(q_ref[last * C:seq, :], last_slab, sb_scr, 0)
    scores(q_ref[seq:ltok, :], 0, sb_scr, 1)
    pair_finish(n_pairs - 1, sa_scr)
    o_ref[last * C:seq, :] = finish(C, last_slab, (MASK_NONE, MASK_META, MASK_UPPER, MASK_ALL), sb_scr, 0)
    o_ref[seq:ltok, :] = finish(N_META, 0, (MASK_METAQ, MASK_NONE, MASK_META, MASK_NONE), sb_scr, 1)


def _attention(mix, sink, batch, ltok):
    n_blocks = (ltok - N_META) // CHUNK
    assert n_blocks % 4 == 0
    slab_rows = (n_blocks + n_blocks // 2) * CHUNK
    stacked = ATT_GROUP * CHUNK
    return pl.pallas_call(
        functools.partial(_att_kernel, n_blocks=n_blocks),
        grid=(batch, ATT_KV_HEADS),
        in_specs=[
            pl.BlockSpec(memory_space=pltpu.SMEM),
            pl.BlockSpec((ltok, 2 * LANES), lambda b, g: (b, OFF_QA // (2 * LANES) + g)),
            pl.BlockSpec((ltok, LANES), lambda b, g: (b, OFF_KA // LANES)),
            pl.BlockSpec((ltok, LANES), lambda b, g: (b, OFF_VA // LANES)),
        ],
        out_specs=pl.BlockSpec((ltok, 2 * LANES), lambda b, g: (b, g)),
        out_shape=jax.ShapeDtypeStruct((batch * ltok, ATT_Q), BF16),
        scratch_shapes=[
            pltpu.VMEM((slab_rows, LANES), BF16),
            pltpu.VMEM((slab_rows, LANES), BF16),
            pltpu.VMEM((N_MASKS, CHUNK, LANES), F32),
            pltpu.VMEM((2, stacked, KEYS), F32),
            pltpu.VMEM((2, stacked, KEYS), F32),
            pltpu.VMEM((4, stacked, KEYS), BF16),
            pltpu.VMEM((4, stacked, LANES), F32),
        ],
        compiler_params=pltpu.CompilerParams(dimension_semantics=("arbitrary", "arbitrary"),
                                             vmem_limit_bytes=VMEM_LIMIT),
        name="attention",
    )(sink, mix, mix, mix)


POST_TM = 688
POST_SUB = ((0, 352), (352, 336))
FF_CHUNK = 1024


def _post_ffn_kernel(h_ref, yr_ref, ya_ref, gt_ref, wr_ref, wa_ref, wm_ref, w1_ref, w2_ref,
                     gpost_ref, gpre_ref, gffpost_ref, o_ref):
    def chain(r):
        y_r = _dot(yr_ref[r, :], wr_ref[...])
        y_a = _dot(ya_ref[r, :], wa_ref[...])
        z = gt_ref[r, :D_MODEL].astype(F32) * y_r + gt_ref[r, D_MODEL:].astype(F32) * y_a
        yield
        mix = _dot(z.astype(BF16), wm_ref[...])
        yield
        h1 = h_ref[r, :] + _rms(mix, gpost_ref[...])
        u = _rms(h1, gpre_ref[...]).astype(BF16)
        yield
        ff = jnp.zeros(h1.shape, F32)
        for c0 in range(0, D_FF, FF_CHUNK):
            hid = jnp.maximum(_dot(u, w1_ref[:, c0:c0 + FF_CHUNK]), 0.0)
            ff = ff + _dot((hid * hid).astype(BF16), w2_ref[c0:c0 + FF_CHUNK, :])
            yield
        o_ref[r, :] = h1 + _rms(ff, gffpost_ref[...])

    chains = [chain(pl.ds(start, rows)) for start, rows in POST_SUB]
    while chains:
        chains = [c for c in chains if next(c, "done") != "done"]


def _post_ffn(h, y_r, y_a, gates, wr, wa, wm, w1, w2, gpost, gpre, gffpost, layer, batch, ltok, out_rows):
    tiles = ltok // POST_TM
    tok = lambda cols: pl.BlockSpec((POST_TM, cols), lambda b, i: (b * tiles + i, 0))
    res = lambda shape: _layer_resident(shape, layer)
    return pl.pallas_call(
        _post_ffn_kernel,
        grid=(batch, tiles),
        in_specs=[tok(D_MODEL), tok(RET_V), tok(ATT_Q), tok(GATE_COLS),
                  res((RET_V, D_MODEL)), res((ATT_Q, D_MODEL)), res((D_MODEL, D_MODEL)),
                  res((D_MODEL, D_FF)), res((D_FF, D_MODEL)),
                  res((1, D_MODEL)), res((1, D_MODEL)), res((1, D_MODEL))],
        out_specs=pl.BlockSpec((None, POST_TM, D_MODEL), lambda b, i: (b, i, 0)),
        out_shape=jax.ShapeDtypeStruct((batch, out_rows, D_MODEL), F32),
        compiler_params=pltpu.CompilerParams(dimension_semantics=("arbitrary", "arbitrary"),
                                             vmem_limit_bytes=VMEM_LIMIT),
        name="post_ffn",
    )(h, y_r, y_a, gates, wr, wa, wm, w1, w2, gpost, gpre, gffpost)


def kernel(x, meta_tokens, w_in, w_ret_o, w_att_o, w_mix_o, w_ff1, w_ff2,
           norm_mix_pre, norm_mix_post, norm_ff_pre, norm_ff_post, ret_decay, attn_sink):
    batch, seq, d = x.shape
    depth = w_in.shape[0]
    ltok = seq + N_META
    assert d == D_MODEL and seq % (2 * CHUNK) == 0 and seq >= BAND and ltok % IN_TM == 0 and ltok % POST_TM == 0
    meta = jnp.broadcast_to(meta_tokens[None].astype(x.dtype), (batch, N_META, d))
    h = jnp.concatenate([x, meta], axis=1).reshape(batch * ltok, d)
    tables = _rotary_tables(seq)
    w_in_b, w_ret_b, w_att_b, w_mix_b, w_ff1_b, w_ff2_b = [
        w.astype(BF16) for w in (w_in, w_ret_o, w_att_o, w_mix_o, w_ff1, w_ff2)]
    gains = [g.astype(F32).reshape(depth, 1, d) for g in (norm_mix_pre, norm_mix_post, norm_ff_pre, norm_ff_post)]
    for l in range(depth):
        mix, gates = _in_proj(h, gains[0], w_in_b, tables, l, ltok // IN_TM)
        y_r = _retention(mix, _decay_lanes(ret_decay[l]), batch, ltok)
        y_a = _attention(mix, attn_sink[l].astype(F32), batch, ltok)
        last = l == depth - 1
        h = _post_ffn(h, y_r, y_a, gates, w_ret_b, w_att_b, w_mix_b, w_ff1_b, w_ff2_b,
                      gains[1], gains[2], gains[3], l, batch, ltok, seq if last else ltok)
        if not last:
            h = h.reshape(batch * ltok, d)
    return h
```

```python
import functools

import jax
import jax.numpy as jnp
from jax import lax
from jax.experimental import pallas as pl
from jax.experimental.pallas import tpu as pltpu

F32 = jnp.float32
BF16 = jnp.bfloat16

D_MODEL = 1024
N_META = 16
CHUNK = 128
RET_HEADS = 8
RET_QK_DIM = 64
RET_V_DIM = 128
ATT_Q_HEADS = 8
ATT_KV_HEADS = 2
ATT_GROUP = ATT_Q_HEADS // ATT_KV_HEADS
ATT_HEAD_DIM = 64
ROPE_DIM = 16
ROPE_THETA = 500000.0
XPOS_THETA = 10000.0
D_FF = 4 * D_MODEL
EPS = 1e-6
NEG_INF = -1e30

RET_QK = RET_HEADS * RET_QK_DIM
RET_V = RET_HEADS * RET_V_DIM
ATT_Q = ATT_Q_HEADS * ATT_HEAD_DIM
ATT_KV = ATT_KV_HEADS * ATT_HEAD_DIM
OFF_QR = 0
OFF_KR = OFF_QR + RET_QK
OFF_VR = OFF_KR + RET_QK
OFF_GR = OFF_VR + RET_V
OFF_QA = OFF_GR + RET_V
OFF_KA = OFF_QA + ATT_Q
OFF_VA = OFF_KA + ATT_KV
MIX_COLS = OFF_VA + ATT_KV
GATE_COLS = 2 * D_MODEL
D_IN = MIX_COLS + GATE_COLS

LANES = 128
MXU_N = 256
VMEM_LIMIT = 60 * 1024 * 1024


def _dot(a, b):
    return jnp.dot(a, b, preferred_element_type=F32)


def _dot_nt(a, b):
    return lax.dot_general(a, b, (((1,), (1,)), ((), ())), preferred_element_type=F32)


def _rms(x, g):
    return x * lax.rsqrt(jnp.mean(x * x, axis=-1, keepdims=True) + EPS) * g


def _sigmoid(x):
    return 1.0 / (1.0 + jnp.exp(-x))


def _resident(shape):
    return pl.BlockSpec(shape, lambda *_: (0,) * len(shape), pipeline_mode=pl.Buffered(1))


def _cast_riders(weights, layer, steps, step_index):
    in_specs, out_specs, out_shapes = [], [], []
    for w in weights:
        _, k, n = w.shape
        rows = k // steps
        assert rows * steps == k and rows % 16 == 0, (k, steps)
        in_specs.append(pl.BlockSpec((None, rows, n), lambda *g: (layer, step_index(*g), 0)))
        out_specs.append(pl.BlockSpec((rows, n), lambda *g: (step_index(*g), 0)))
        out_shapes.append(jax.ShapeDtypeStruct((k, n), BF16))
    return in_specs, out_specs, out_shapes


def _split_refs(refs, n_in, n_cast):
    a, b = n_in + n_cast, n_in + n_cast + 1
    return refs[:n_in], refs[n_in:a], refs[a], refs[b:b + n_cast], refs[b + n_cast:]


def _run_cast_riders(cast_in, cast_out):
    for src, dst in zip(cast_in, cast_out):
        dst[...] = src[...].astype(BF16)


def _layer_resident(shape, layer):
    return pl.BlockSpec((None,) + shape, lambda *_: (layer,) + (0,) * len(shape), pipeline_mode=pl.Buffered(1))


def _rotary_tables(seq):
    pos = jnp.concatenate([jnp.arange(N_META, N_META + seq), jnp.arange(N_META)]).astype(F32)
    d = jnp.arange(LANES) % RET_QK_DIM

    def table(theta, rot_dim):
        half = rot_dim // 2
        freqs = jnp.power(jnp.float32(theta), -jnp.arange(0, rot_dim, 2, dtype=F32) / rot_dim)
        ang = pos[:, None] * freqs[None, :]
        cos, sin = jnp.cos(ang), jnp.sin(ang)
        idx = d % half
        rotated = (d < rot_dim)[None, :]
        cos_l = jnp.where(rotated, cos[:, idx], 1.0)
        sin_l = jnp.where(rotated, jnp.where((d < half)[None, :], -sin[:, idx], sin[:, idx]), 0.0)
        return cos_l, sin_l

    cr, sr = table(XPOS_THETA, RET_QK_DIM)
    ca, sa = table(ROPE_THETA, ROPE_DIM)
    qs_r = RET_QK_DIM ** -0.5
    qs_a = ATT_HEAD_DIM ** -0.5
    return (cr * qs_r, sr * qs_r, cr, sr, ca * qs_a, sa * qs_a, ca, sa)


def _rotate(x, cos, sin, half):
    lane = lax.broadcasted_iota(jnp.int32, x.shape, 1) % RET_QK_DIM
    partner = jnp.where(lane < half, pltpu.roll(x, LANES - half, 1), pltpu.roll(x, half, 1))
    return x * cos + partner * sin


IN_TM = 688
IN_SUB = ((0, 352), (352, 336))


def _in_proj_kernel(h_ref, g_ref, w_ref, crq, srq, crk, srk, caq, saq, cak, sak, mix_ref, gate_ref):
    def chain(r):
        u = _rms(h_ref[r, :], g_ref[...]).astype(BF16)
        yield

        def proj(c0, n):
            return _dot(u, w_ref[:, c0:c0 + n])

        def rot_store(c0, cos_ref, sin_ref, half):
            p = proj(c0, MXU_N)
            for j in range(MXU_N // LANES):
                t = _rotate(p[:, j * LANES:(j + 1) * LANES], cos_ref[r, :], sin_ref[r, :], half)
                mix_ref[r, c0 + j * LANES:c0 + (j + 1) * LANES] = t.astype(BF16)

        for c0 in range(OFF_QR, OFF_KR, MXU_N):
            rot_store(c0, crq, srq, RET_QK_DIM // 2)
            yield
        for c0 in range(OFF_KR, OFF_VR, MXU_N):
            rot_store(c0, crk, srk, RET_QK_DIM // 2)
            yield
        for c0 in range(OFF_VR, OFF_GR, MXU_N):
            mix_ref[r, c0:c0 + MXU_N] = proj(c0, MXU_N).astype(BF16)
            yield
        for c0 in range(OFF_GR, OFF_QA, MXU_N):
            p = proj(c0, MXU_N)
            mix_ref[r, c0:c0 + MXU_N] = (p * _sigmoid(p)).astype(BF16)
            yield
        for c0 in range(OFF_QA, OFF_KA, MXU_N):
            rot_store(c0, caq, saq, ROPE_DIM // 2)
            yield
        p = proj(OFF_KA, MXU_N)
        mix_ref[r, OFF_KA:OFF_VA] = _rotate(p[:, :LANES], cak[r, :], sak[r, :], ROPE_DIM // 2).astype(BF16)
        mix_ref[r, OFF_VA:MIX_COLS] = p[:, LANES:].astype(BF16)
        yield
        for c0 in range(0, GATE_COLS, MXU_N):
            gate_ref[r, c0:c0 + MXU_N] = _sigmoid(proj(MIX_COLS + c0, MXU_N)).astype(BF16)
            yield

    chains = [chain(pl.ds(start, rows)) for start, rows in IN_SUB]
    while chains:
        chains = [c for c in chains if next(c, "done") != "done"]


def _in_proj(h, gains, w, tables, layer, tiles_per_row):
    t = h.shape[0]
    tok = lambda cols: pl.BlockSpec((IN_TM, cols), lambda i: (i, 0))
    tab = pl.BlockSpec((IN_TM, LANES), lambda i: (i % tiles_per_row, 0))
    return pl.pallas_call(
        _in_proj_kernel,
        grid=(t // IN_TM,),
        in_specs=[tok(D_MODEL), _layer_resident((1, D_MODEL), layer), _resident((D_MODEL, D_IN))]
                 + [tab] * 8,
        out_specs=[tok(MIX_COLS), tok(GATE_COLS)],
        out_shape=[jax.ShapeDtypeStruct((t, MIX_COLS), BF16), jax.ShapeDtypeStruct((t, GATE_COLS), BF16)],
        compiler_params=pltpu.CompilerParams(dimension_semantics=("arbitrary",), vmem_limit_bytes=VMEM_LIMIT),
        name="in_proj",
    )(h, gains, w, *tables)


def _ret_kernel(*refs, n_chunks, n_cast):
    (rd_ref, q_ref, k_ref, v_ref, sg_ref), cast_in, o_ref, cast_out, scratch = _split_refs(refs, 5, n_cast)
    tab_scr, dec_scr, pad_scr, padw_scr, kv_scr, st_scr, s_scr = scratch
    _run_cast_riders(cast_in, cast_out)
    C = CHUNK
    seq = n_chunks * C
    W = 2 * LANES
    H = RET_QK_DIM
    lg = -jnp.exp(rd_ref[...])
    lgf_a, lgf_b, lgb_a, lgb_b, lgf_p, lgb_p = [lg[i:i + 1, :] for i in range(6)]

    row = lax.broadcasted_iota(jnp.int32, (C, LANES), 0)
    col = lax.broadcasted_iota(jnp.int32, (C, LANES), 1)
    a = row.astype(F32)
    rel = (row - col).astype(F32)

    def decay_matrix(lf, lb):
        return jnp.where(rel >= 0, jnp.exp(lf * jnp.maximum(rel, 0.0)), jnp.exp(lb * jnp.maximum(-rel, 0.0)))

    tab_scr[0] = jnp.concatenate([jnp.exp(lgf_p * (a + 1.0)), jnp.exp(lgb_p * (C - a))], axis=1)
    tab_scr[1] = jnp.concatenate([jnp.exp(lgf_p * (C - 1.0 - a)), jnp.exp(lgb_p * a)], axis=1)
    tab_scr[2] = jnp.concatenate([decay_matrix(lgf_a, lgb_a), decay_matrix(lgf_b, lgb_b)], axis=1)
    top = row < H
    for i, (la, lb) in enumerate(((lgf_a, lgf_b), (lgb_a, lgb_b))):
        g = jnp.where(top, jnp.exp(la * C), jnp.exp(lb * C))
        dec_scr[i] = jnp.concatenate([g, g], axis=1)
    lo = col < H

    pad_rows = C - N_META
    for i, ref in enumerate((q_ref, k_ref)):
        pad_scr[i, 0:pad_rows, :] = jnp.zeros((pad_rows, LANES), BF16)
        pad_scr[i, pad_rows:C, :] = ref[seq:seq + N_META, :]
    for i, ref in enumerate((v_ref, sg_ref)):
        padw_scr[i, 0:pad_rows, :] = jnp.zeros((pad_rows, W), BF16)
        padw_scr[i, pad_rows:C, :] = ref[seq:seq + N_META, :]

    def seq_rows(c):
        return pl.ds(c * C, C)

    def kv(kc, vc):
        kf = kc.astype(F32)
        ks = jnp.concatenate([kf, kf], axis=1) * tab_scr[1]
        kt = jnp.concatenate([ks[:, :LANES].T, ks[:, LANES:].T], axis=0).astype(BF16)
        x = _dot(kt, vc)
        z = jnp.zeros((H, LANES), F32)
        rows = []
        for d in range(2):
            rows.append(jnp.concatenate([x[d * C:d * C + H, :LANES], z], axis=1))
            rows.append(jnp.concatenate([z, x[d * C + H:(d + 1) * C, LANES:]], axis=1))
        return jnp.concatenate(rows, axis=0)

    kv_scr[0] = kv(pad_scr[1], padw_scr[0])
    for c in range(n_chunks):
        r = seq_rows(c)
        kv_scr[c + 1] = kv(k_ref[r, :], v_ref[r, :])

    s_scr[...] = jnp.zeros((2, C, W), F32)
    for t in range(n_chunks + 1):
        u = n_chunks - t
        sf = s_scr[0]
        sb = s_scr[1]
        st_scr[t, 0:C, :] = sf.astype(BF16)
        st_scr[u, C:2 * C, :] = sb.astype(BF16)
        s_scr[0] = sf * dec_scr[0] + kv_scr[t, 0:C, :]
        s_scr[1] = sb * dec_scr[1] + kv_scr[u, C:2 * C, :]

    def emit(qc, kc, vc, sgc, st):
        qf = qc.astype(F32)
        kf = kc.astype(F32)
        k2 = jnp.concatenate([jnp.where(lo, kf, 0.0), jnp.where(lo, 0.0, kf)], axis=0).astype(BF16)
        p = (_dot_nt(qc, k2) * tab_scr[2]).astype(BF16)
        zb = jnp.zeros((C, LANES), BF16)
        vbd = jnp.concatenate([jnp.concatenate([vc[:, :LANES], zb], axis=1),
                               jnp.concatenate([zb, vc[:, LANES:]], axis=1)], axis=0)
        qs = (jnp.concatenate([qf, qf], axis=1) * tab_scr[0]).astype(BF16)
        o = _dot(p, vbd) + _dot(qs, st)
        halves = []
        for j in range(2):
            oh = o[:, j * LANES:(j + 1) * LANES]
            mu = jnp.mean(oh, axis=-1, keepdims=True)
            var = jnp.mean(jnp.square(oh - mu), axis=-1, keepdims=True)
            halves.append((oh - mu) * lax.rsqrt(var + EPS))
        return (jnp.concatenate(halves, axis=1) * sgc.astype(F32)).astype(BF16)

    for c in range(n_chunks):
        r = seq_rows(c)
        o_ref[r, :] = emit(q_ref[r, :], k_ref[r, :], v_ref[r, :], sg_ref[r, :], st_scr[c + 1])
    om = emit(pad_scr[0], pad_scr[1], padw_scr[0], padw_scr[1], st_scr[0])
    o_ref[seq:seq + N_META, :] = om[pad_rows:, :]


def _retention(mix, rd_lanes, batch, ltok, cast_weights, layer):
    n_chunks = (ltok - N_META) // CHUNK
    pairs = RET_HEADS // 2
    blk = lambda cols, base: pl.BlockSpec((ltok, cols), lambda b, j: (b, base + j))
    c_in, c_out, c_shapes = _cast_riders(cast_weights, layer, batch * pairs, lambda b, j: b * pairs + j)
    return pl.pallas_call(
        functools.partial(_ret_kernel, n_chunks=n_chunks, n_cast=len(cast_weights)),
        grid=(batch, pairs),
        in_specs=[
            pl.BlockSpec((None, 8, LANES), lambda b, j: (j, 0, 0)),
            blk(LANES, OFF_QR // LANES),
            blk(LANES, OFF_KR // LANES),
            blk(2 * LANES, OFF_VR // (2 * LANES)),
            blk(2 * LANES, OFF_GR // (2 * LANES)),
        ] + c_in,
        out_specs=[pl.BlockSpec((ltok, 2 * LANES), lambda b, j: (b, j))] + c_out,
        out_shape=[jax.ShapeDtypeStruct((batch * ltok, RET_V), BF16)] + c_shapes,
        scratch_shapes=[
            pltpu.VMEM((3, CHUNK, 2 * LANES), F32),
            pltpu.VMEM((2, CHUNK, 2 * LANES), F32),
            pltpu.VMEM((2, CHUNK, LANES), BF16),
            pltpu.VMEM((2, CHUNK, 2 * LANES), BF16),
            pltpu.VMEM((n_chunks + 1, 2 * CHUNK, 2 * LANES), F32),
            pltpu.VMEM((n_chunks + 1, 2 * CHUNK, 2 * LANES), BF16),
            pltpu.VMEM((2, CHUNK, 2 * LANES), F32),
        ],
        compiler_params=pltpu.CompilerParams(dimension_semantics=("arbitrary", "arbitrary"),
                                             vmem_limit_bytes=VMEM_LIMIT),
        name="retention",
    )(rd_lanes, mix, mix, mix, mix, *cast_weights)


def _decay_lanes(ret_decay_l):
    rd = ret_decay_l.astype(F32)
    pairs = RET_HEADS // 2
    fa, fb = rd[0, 0::2], rd[0, 1::2]
    ba, bb = rd[1, 0::2], rd[1, 1::2]
    full = lambda v: jnp.broadcast_to(v[:, None], (pairs, LANES))
    split = lambda x, y: jnp.concatenate([jnp.broadcast_to(x[:, None], (pairs, RET_QK_DIM)),
                                          jnp.broadcast_to(y[:, None], (pairs, RET_QK_DIM))], axis=1)
    rows = [full(fa), full(fb), full(ba), full(bb), split(fa, fb), split(ba, bb), full(fa), full(fa)]
    return jnp.stack(rows, axis=1)


BAND = 3 * CHUNK
KEYS = BAND + CHUNK


MASK_UPPER, MASK_LOWER, MASK_META, MASK_METAQ, MASK_NONE = range(5)
MASK_ALL = None
N_MASKS = 5
SOFTMAX_ROWS = 32


def _att_kernel(*refs, n_blocks, n_cast):
    (sink_ref, q_ref, k_ref, v_ref), cast_in, o_ref, cast_out, scratch = _split_refs(refs, 4, n_cast)
    k_scr, v_scr, bias_scr, sa_scr, sb_scr, e_scr, t_scr = scratch
    _run_cast_riders(cast_in, cast_out)
    C = CHUNK
    seq = n_blocks * C
    g = pl.program_id(1)
    ltok = seq + N_META
    lane = lax.broadcasted_iota(jnp.int32, (ltok, LANES), 1)
    lo = lane < ATT_HEAD_DIM
    first = g == 0

    def own_head_low(ref):
        x = ref[...].astype(F32)
        return jnp.where(first, x, pltpu.roll(x, ATT_HEAD_DIM, 1))

    kk = own_head_low(k_ref)
    kdup = jnp.where(lo, kk, pltpu.roll(kk, ATT_HEAD_DIM, 1)).astype(BF16)
    vone = jnp.where(lo, own_head_low(v_ref), 1.0).astype(BF16)
    pad = jnp.zeros((C - N_META, LANES), BF16)
    for scr, val in ((k_scr, kdup), (v_scr, vone)):
        meta_tile = jnp.concatenate([val[seq:ltok, :], pad], axis=0)
        for j in range(n_blocks):
            t = j + j // 2
            scr[t * C:(t + 1) * C, :] = val[j * C:(j + 1) * C, :]
            if j % 2 == 1:
                scr[(t + 1) * C:(t + 2) * C, :] = meta_tile

    row = lax.broadcasted_iota(jnp.int32, (C, LANES), 0)
    col = lax.broadcasted_iota(jnp.int32, (C, LANES), 1)
    visible = lambda cond: jnp.where(cond, 0.0, NEG_INF)
    bias_scr[MASK_UPPER] = visible(col >= row)
    bias_scr[MASK_LOWER] = visible(col <= row)
    bias_scr[MASK_META] = visible(col < N_META)
    bias_scr[MASK_METAQ] = visible(col <= row + (C - N_META))
    bias_scr[MASK_NONE] = visible(col < 0)

    def scores(q4, slab_start, s_scr, slot):
        R = q4.shape[0]
        qf = q4.astype(F32)
        lo_r = lax.broadcasted_iota(jnp.int32, (R, LANES), 1) < ATT_HEAD_DIM
        stacked = []
        for p in range(ATT_GROUP // 2):
            qp = qf[:, p * LANES:(p + 1) * LANES]
            stacked += [jnp.where(lo_r, qp, 0.0), jnp.where(lo_r, 0.0, qp)]
        qs = jnp.concatenate(stacked, axis=0).astype(BF16)
        s_scr[slot, 0:ATT_GROUP * R, :] = _dot_nt(qs, k_scr[pl.ds(slab_start, KEYS), :])

    def finish(R, slab_start, masks, s_scr, slot):
        eslot = slot if s_scr is sa_scr else 2 + slot
        SR = min(R, SOFTMAX_ROWS)
        lo_r = lax.broadcasted_iota(jnp.int32, (R, LANES), 1) < ATT_HEAD_DIM
        slab = pl.ds(slab_start, KEYS)
        for h in range(ATT_GROUP):
            sink = sink_ref[g * ATT_GROUP + h]
            for i in range(R // SR):
                rows = pl.ds(h * R + i * SR, SR)
                tiles = []
                for t, mask in enumerate(masks):
                    x = s_scr[slot, rows, t * LANES:(t + 1) * LANES]
                    if mask is not MASK_ALL:
                        x = x + bias_scr[mask, i * SR:(i + 1) * SR, :]
                    tiles.append(x)
                top = jnp.maximum(jnp.maximum(tiles[0], tiles[1]), jnp.maximum(tiles[2], tiles[3]))
                m = jnp.maximum(jnp.max(top, axis=-1, keepdims=True), sink)
                for t in range(len(masks)):
                    e_scr[eslot, rows, t * LANES:(t + 1) * LANES] = jnp.exp(tiles[t] - m).astype(BF16)
                t_scr[eslot, rows, :] = jnp.broadcast_to(jnp.exp(sink - m), (SR, LANES))
        res = _dot(e_scr[eslot, 0:ATT_GROUP * R, :], v_scr[slab, :])
        outs = []
        for p in range(ATT_GROUP // 2):
            ev, od = pl.ds(2 * p * R, R), pl.ds((2 * p + 1) * R, R)
            r_ev, r_od = res[2 * p * R:(2 * p + 1) * R, :], res[(2 * p + 1) * R:(2 * p + 2) * R, :]
            num = jnp.where(lo_r, r_ev, pltpu.roll(r_od, ATT_HEAD_DIM, 1))
            den = (jnp.where(lo_r, pltpu.roll(r_ev, ATT_HEAD_DIM, 1), r_od)
                   + jnp.where(lo_r, t_scr[eslot, ev, :], t_scr[eslot, od, :]))
            outs.append(num / den)
        return jnp.concatenate(outs, axis=1).astype(BF16)

    odd_masks = (MASK_UPPER, MASK_ALL, MASK_META, MASK_LOWER)
    even_masks = (MASK_UPPER, MASK_META, MASK_ALL, MASK_LOWER)
    n_pairs = n_blocks // 2 - 1

    def q_rows(n):
        return pl.ds(pl.multiple_of(n * C, C), C)

    def pair_scores(i, s_scr):
        scores(q_ref[q_rows(2 * i + 1), :], pl.multiple_of(3 * i * C, C), s_scr, 0)
        scores(q_ref[q_rows(2 * i + 2), :], pl.multiple_of((3 * i + 1) * C, C), s_scr, 1)

    def pair_finish(i, s_scr):
        o_ref[q_rows(2 * i + 1), :] = finish(C, pl.multiple_of(3 * i * C, C), odd_masks, s_scr, 0)
        o_ref[q_rows(2 * i + 2), :] = finish(C, pl.multiple_of((3 * i + 1) * C, C), even_masks, s_scr, 1)

    pair_scores(0, sa_scr)
    scores(q_ref[0:C, :], 0, sb_scr, 0)
    o_ref[0:C, :] = finish(C, 0, (MASK_ALL, MASK_LOWER, MASK_META, MASK_NONE), sb_scr, 0)

    def body(j, carry):
        pair_scores(2 * j + 1, sb_scr)
        pair_finish(2 * j, sa_scr)
        pair_scores(2 * j + 2, sa_scr)
        pair_finish(2 * j + 1, sb_scr)
        return carry

    lax.fori_loop(0, (n_pairs - 1) // 2, body, 0)
    last = n_blocks - 1
    last_slab = ((last - 2) + (last - 2) // 2) * C
    scores(q_ref[last * C:seq, :], last_slab, sb_scr, 0)
    scores(q_ref[seq:ltok, :], 0, sb_scr, 1)
    pair_finish(n_pairs - 1, sa_scr)
    o_ref[last * C:seq, :] = finish(C, last_slab, (MASK_NONE, MASK_META, MASK_UPPER, MASK_ALL), sb_scr, 0)
    o_ref[seq:ltok, :] = finish(N_META, 0, (MASK_METAQ, MASK_NONE, MASK_META, MASK_NONE), sb_scr, 1)


def _attention(mix, sink, batch, ltok, cast_weights, layer):
    n_blocks = (ltok - N_META) // CHUNK
    assert n_blocks % 4 == 0
    slab_rows = (n_blocks + n_blocks // 2) * CHUNK
    stacked = ATT_GROUP * CHUNK
    c_in, c_out, c_shapes = _cast_riders(cast_weights, layer, batch * ATT_KV_HEADS,
                                         lambda b, g: b * ATT_KV_HEADS + g)
    return pl.pallas_call(
        functools.partial(_att_kernel, n_blocks=n_blocks, n_cast=len(cast_weights)),
        grid=(batch, ATT_KV_HEADS),
        in_specs=[
            pl.BlockSpec(memory_space=pltpu.SMEM),
            pl.BlockSpec((ltok, 2 * LANES), lambda b, g: (b, OFF_QA // (2 * LANES) + g)),
            pl.BlockSpec((ltok, LANES), lambda b, g: (b, OFF_KA // LANES)),
            pl.BlockSpec((ltok, LANES), lambda b, g: (b, OFF_VA // LANES)),
        ] + c_in,
        out_specs=[pl.BlockSpec((ltok, 2 * LANES), lambda b, g: (b, g))] + c_out,
        out_shape=[jax.ShapeDtypeStruct((batch * ltok, ATT_Q), BF16)] + c_shapes,
        scratch_shapes=[
            pltpu.VMEM((slab_rows, LANES), BF16),
            pltpu.VMEM((slab_rows, LANES), BF16),
            pltpu.VMEM((N_MASKS, CHUNK, LANES), F32),
            pltpu.VMEM((2, stacked, KEYS), F32),
            pltpu.VMEM((2, stacked, KEYS), F32),
            pltpu.VMEM((4, stacked, KEYS), BF16),
            pltpu.VMEM((4, stacked, LANES), F32),
        ],
        compiler_params=pltpu.CompilerParams(dimension_semantics=("arbitrary", "arbitrary"),
                                             vmem_limit_bytes=VMEM_LIMIT),
        name="attention",
    )(sink, mix, mix, mix, *cast_weights)


POST_TM = 688
POST_SUB = ((0, 352), (352, 336))
FF_CHUNK = 1024


def _post_ffn_kernel(h_ref, yr_ref, ya_ref, gt_ref, wr_ref, wa_ref, wm_ref, w1_ref, w2_ref,
                     gpost_ref, gpre_ref, gffpost_ref, o_ref):
    def chain(r):
        y_r = _dot(yr_ref[r, :], wr_ref[...])
        y_a = _dot(ya_ref[r, :], wa_ref[...])
        z = gt_ref[r, :D_MODEL].astype(F32) * y_r + gt_ref[r, D_MODEL:].astype(F32) * y_a
        yield
        mix = _dot(z.astype(BF16), wm_ref[...])
        yield
        h1 = h_ref[r, :] + _rms(mix, gpost_ref[...])
        u = _rms(h1, gpre_ref[...]).astype(BF16)
        yield
        ff = jnp.zeros(h1.shape, F32)
        for c0 in range(0, D_FF, FF_CHUNK):
            hid = jnp.maximum(_dot(u, w1_ref[:, c0:c0 + FF_CHUNK]), 0.0)
            ff = ff + _dot((hid * hid).astype(BF16), w2_ref[c0:c0 + FF_CHUNK, :])
            yield
        o_ref[r, :] = h1 + _rms(ff, gffpost_ref[...])

    chains = [chain(pl.ds(start, rows)) for start, rows in POST_SUB]
    while chains:
        chains = [c for c in chains if next(c, "done") != "done"]


def _post_ffn(h, y_r, y_a, gates, wr, wa, wm, w1, w2, gpost, gpre, gffpost, layer, batch, ltok, out_rows):
    tiles = ltok // POST_TM
    tok = lambda cols: pl.BlockSpec((POST_TM, cols), lambda b, i: (b * tiles + i, 0))
    gain = _layer_resident((1, D_MODEL), layer)
    return pl.pallas_call(
        _post_ffn_kernel,
        grid=(batch, tiles),
        in_specs=[tok(D_MODEL), tok(RET_V), tok(ATT_Q), tok(GATE_COLS),
                  _resident((RET_V, D_MODEL)), _resident((ATT_Q, D_MODEL)), _resident((D_MODEL, D_MODEL)),
                  _resident((D_MODEL, D_FF)), _resident((D_FF, D_MODEL)), gain, gain, gain],
        out_specs=pl.BlockSpec((None, POST_TM, D_MODEL), lambda b, i: (b, i, 0)),
        out_shape=jax.ShapeDtypeStruct((batch, out_rows, D_MODEL), F32),
        compiler_params=pltpu.CompilerParams(dimension_semantics=("arbitrary", "arbitrary"),
                                             vmem_limit_bytes=VMEM_LIMIT),
        name="post_ffn",
    )(h, y_r, y_a, gates, wr, wa, wm, w1, w2, gpost, gpre, gffpost)


def kernel(x, meta_tokens, w_in, w_ret_o, w_att_o, w_mix_o, w_ff1, w_ff2,
           norm_mix_pre, norm_mix_post, norm_ff_pre, norm_ff_post, ret_decay, attn_sink):
    batch, seq, d = x.shape
    depth = w_in.shape[0]
    ltok = seq + N_META
    assert d == D_MODEL and seq % (2 * CHUNK) == 0 and seq >= BAND and ltok % IN_TM == 0 and ltok % POST_TM == 0
    meta = jnp.broadcast_to(meta_tokens[None].astype(x.dtype), (batch, N_META, d))
    h = jnp.concatenate([x, meta], axis=1).reshape(batch * ltok, d)
    tables = _rotary_tables(seq)
    gains = [g.astype(F32).reshape(depth, 1, d) for g in (norm_mix_pre, norm_mix_post, norm_ff_pre, norm_ff_post)]
    post_weights = tuple(w.astype(F32) for w in (w_ret_o, w_att_o, w_mix_o, w_ff1, w_ff2))
    w_in_f = w_in.astype(F32)
    w_in_b = w_in_f[0].astype(BF16)
    for l in range(depth):
        last = l == depth - 1
        mix, gates = _in_proj(h, gains[0], w_in_b, tables, l, ltok // IN_TM)
        y_r, *post_b = _retention(mix, _decay_lanes(ret_decay[l]), batch, ltok, post_weights, l)
        y_a, *next_in = _attention(mix, attn_sink[l].astype(F32), batch, ltok, () if last else (w_in_f,), l + 1)
        if not last:
            w_in_b, = next_in
        h = _post_ffn(h, y_r, y_a, gates, *post_b,
                      gains[1], gains[2], gains[3], l, batch, ltok, seq if last else ltok)
        if not last:
            h = h.reshape(batch * ltok, d)
    return h
```

```python
import functools

import jax
import jax.numpy as jnp
from jax import lax
from jax.experimental import pallas as pl
from jax.experimental.pallas import tpu as pltpu

F32 = jnp.float32
BF16 = jnp.bfloat16

D_MODEL = 1024
N_META = 16
CHUNK = 128
RET_HEADS = 8
RET_QK_DIM = 64
RET_V_DIM = 128
ATT_Q_HEADS = 8
ATT_KV_HEADS = 2
ATT_GROUP = ATT_Q_HEADS // ATT_KV_HEADS
ATT_HEAD_DIM = 64
ROPE_DIM = 16
ROPE_THETA = 500000.0
XPOS_THETA = 10000.0
D_FF = 4 * D_MODEL
EPS = 1e-6
NEG_INF = -1e30

RET_QK = RET_HEADS * RET_QK_DIM
RET_V = RET_HEADS * RET_V_DIM
ATT_Q = ATT_Q_HEADS * ATT_HEAD_DIM
ATT_KV = ATT_KV_HEADS * ATT_HEAD_DIM
OFF_QR = 0
OFF_KR = OFF_QR + RET_QK
OFF_VR = OFF_KR + RET_QK
OFF_GR = OFF_VR + RET_V
OFF_QA = OFF_GR + RET_V
OFF_KA = OFF_QA + ATT_Q
OFF_VA = OFF_KA + ATT_KV
MIX_COLS = OFF_VA + ATT_KV
GATE_COLS = 2 * D_MODEL
D_IN = MIX_COLS + GATE_COLS

LANES = 128
MXU_N = 256
VMEM_LIMIT = 60 * 1024 * 1024


def _dot(a, b):
    return jnp.dot(a, b, preferred_element_type=F32)


def _dot_nt(a, b):
    return lax.dot_general(a, b, (((1,), (1,)), ((), ())), preferred_element_type=F32)


def _rms(x, g):
    return x * lax.rsqrt(jnp.mean(x * x, axis=-1, keepdims=True) + EPS) * g


def _sigmoid(x):
    return 1.0 / (1.0 + jnp.exp(-x))


def _resident(shape):
    return pl.BlockSpec(shape, lambda *_: (0,) * len(shape), pipeline_mode=pl.Buffered(1))


def _cast_riders(weights, layer, steps, step_index):
    in_specs, out_specs, out_shapes = [], [], []
    for w in weights:
        _, k, n = w.shape
        rows = k // steps
        assert rows * steps == k and rows % 16 == 0, (k, steps)
        in_specs.append(pl.BlockSpec((None, rows, n), lambda *g: (layer, step_index(*g), 0)))
        out_specs.append(pl.BlockSpec((rows, n), lambda *g: (step_index(*g), 0)))
        out_shapes.append(jax.ShapeDtypeStruct((k, n), BF16))
    return in_specs, out_specs, out_shapes


def _split_refs(refs, n_in, n_cast):
    a, b = n_in + n_cast, n_in + n_cast + 1
    return refs[:n_in], refs[n_in:a], refs[a], refs[b:b + n_cast], refs[b + n_cast:]


def _run_cast_riders(cast_in, cast_out):
    for src, dst in zip(cast_in, cast_out):
        dst[...] = src[...].astype(BF16)


def _layer_resident(shape, layer):
    return pl.BlockSpec((None,) + shape, lambda *_: (layer,) + (0,) * len(shape), pipeline_mode=pl.Buffered(1))


def _rotary_tables(seq):
    pos = jnp.concatenate([jnp.arange(N_META, N_META + seq), jnp.arange(N_META)]).astype(F32)
    d = jnp.arange(LANES) % RET_QK_DIM

    def table(theta, rot_dim):
        half = rot_dim // 2
        freqs = jnp.power(jnp.float32(theta), -jnp.arange(0, rot_dim, 2, dtype=F32) / rot_dim)
        ang = pos[:, None] * freqs[None, :]
        cos, sin = jnp.cos(ang), jnp.sin(ang)
        idx = d % half
        rotated = (d < rot_dim)[None, :]
        cos_l = jnp.where(rotated, cos[:, idx], 1.0)
        sin_l = jnp.where(rotated, jnp.where((d < half)[None, :], -sin[:, idx], sin[:, idx]), 0.0)
        return cos_l, sin_l

    cr, sr = table(XPOS_THETA, RET_QK_DIM)
    ca, sa = table(ROPE_THETA, ROPE_DIM)
    qs_r = RET_QK_DIM ** -0.5
    qs_a = ATT_HEAD_DIM ** -0.5
    return (cr * qs_r, sr * qs_r, cr, sr, ca * qs_a, sa * qs_a, ca, sa)


def _rotate(x, cos, sin, half):
    lane = lax.broadcasted_iota(jnp.int32, x.shape, 1) % RET_QK_DIM
    partner = jnp.where(lane < half, pltpu.roll(x, LANES - half, 1), pltpu.roll(x, half, 1))
    return x * cos + partner * sin


IN_TM = 688
IN_SUB = ((0, 272), (272, 416))


def _in_proj_kernel(h_ref, g_ref, w_ref, crq, srq, crk, srk, caq, saq, cak, sak, mix_ref, gate_ref):
    def chain(r):
        u = _rms(h_ref[r, :], g_ref[...]).astype(BF16)
        yield

        def proj(c0, n):
            return _dot(u, w_ref[:, c0:c0 + n])

        def rot_store(c0, cos_ref, sin_ref, half):
            p = proj(c0, MXU_N)
            for j in range(MXU_N // LANES):
                t = _rotate(p[:, j * LANES:(j + 1) * LANES], cos_ref[r, :], sin_ref[r, :], half)
                mix_ref[r, c0 + j * LANES:c0 + (j + 1) * LANES] = t.astype(BF16)

        for c0 in range(OFF_QR, OFF_KR, MXU_N):
            rot_store(c0, crq, srq, RET_QK_DIM // 2)
            yield
        for c0 in range(OFF_KR, OFF_VR, MXU_N):
            rot_store(c0, crk, srk, RET_QK_DIM // 2)
            yield
        for c0 in range(OFF_GR, OFF_QA, MXU_N):
            p = proj(c0, MXU_N)
            mix_ref[r, c0:c0 + MXU_N] = (p * _sigmoid(p)).astype(BF16)
            yield
        for c0 in range(OFF_QA, OFF_KA, MXU_N):
            rot_store(c0, caq, saq, ROPE_DIM // 2)
            yield
        p = proj(OFF_KA, MXU_N)
        mix_ref[r, OFF_KA:OFF_VA] = _rotate(p[:, :LANES], cak[r, :], sak[r, :], ROPE_DIM // 2).astype(BF16)
        mix_ref[r, OFF_VA:MIX_COLS] = p[:, LANES:].astype(BF16)
        yield
        for c0 in range(0, GATE_COLS, MXU_N):
            gate_ref[r, c0:c0 + MXU_N] = _sigmoid(proj(MIX_COLS + c0, MXU_N)).astype(BF16)
            yield
        for c0 in range(OFF_VR, OFF_GR, MXU_N):
            mix_ref[r, c0:c0 + MXU_N] = proj(c0, MXU_N).astype(BF16)
            yield

    chains = [chain(pl.ds(start, rows)) for start, rows in IN_SUB]
    while chains:
        chains = [c for c in chains if next(c, "done") != "done"]


def _in_proj(h, gains, w, tables, layer, tiles_per_row):
    t = h.shape[0]
    tok = lambda cols: pl.BlockSpec((IN_TM, cols), lambda i: (i, 0))
    tab = pl.BlockSpec((IN_TM, LANES), lambda i: (i % tiles_per_row, 0))
    return pl.pallas_call(
        _in_proj_kernel,
        grid=(t // IN_TM,),
        in_specs=[tok(D_MODEL), _layer_resident((1, D_MODEL), layer), _resident((D_MODEL, D_IN))]
                 + [tab] * 8,
        out_specs=[tok(MIX_COLS), tok(GATE_COLS)],
        out_shape=[jax.ShapeDtypeStruct((t, MIX_COLS), BF16), jax.ShapeDtypeStruct((t, GATE_COLS), BF16)],
        compiler_params=pltpu.CompilerParams(dimension_semantics=("arbitrary",), vmem_limit_bytes=VMEM_LIMIT),
        name="in_proj",
    )(h, gains, w, *tables)


RET_PAIRS_PER_STEP = 2


def _ret_kernel(*refs, n_chunks, n_cast):
    (rd_ref, q_ref, k_ref, v_ref, sg_ref), cast_in, o_ref, cast_out, scratch = _split_refs(refs, 5, n_cast)
    _run_cast_riders(cast_in, cast_out)
    for p in range(RET_PAIRS_PER_STEP):
        narrow, wide = pl.ds(p * LANES, LANES), pl.ds(2 * p * LANES, 2 * LANES)
        _ret_pair(rd_ref.at[p], q_ref.at[:, narrow], k_ref.at[:, narrow], v_ref.at[:, wide], sg_ref.at[:, wide],
                  o_ref.at[:, wide], [s.at[p] for s in scratch], n_chunks)


def _ret_pair(rd_ref, q_ref, k_ref, v_ref, sg_ref, o_ref, scratch, n_chunks):
    tab_scr, dec_scr, pad_scr, padw_scr, kv_scr, st_scr, s_scr = scratch
    C = CHUNK
    seq = n_chunks * C
    W = 2 * LANES
    H = RET_QK_DIM
    lg = -jnp.exp(rd_ref[...])
    lgf_a, lgf_b, lgb_a, lgb_b, lgf_p, lgb_p = [lg[i:i + 1, :] for i in range(6)]

    row = lax.broadcasted_iota(jnp.int32, (C, LANES), 0)
    col = lax.broadcasted_iota(jnp.int32, (C, LANES), 1)
    a = row.astype(F32)
    rel = (row - col).astype(F32)

    def decay_matrix(lf, lb):
        return jnp.where(rel >= 0, jnp.exp(lf * jnp.maximum(rel, 0.0)), jnp.exp(lb * jnp.maximum(-rel, 0.0)))

    tab_scr[0] = jnp.concatenate([jnp.exp(lgf_p * (a + 1.0)), jnp.exp(lgb_p * (C - a))], axis=1)
    tab_scr[1] = jnp.concatenate([jnp.exp(lgf_p * (C - 1.0 - a)), jnp.exp(lgb_p * a)], axis=1)
    tab_scr[2] = jnp.concatenate([decay_matrix(lgf_a, lgb_a), decay_matrix(lgf_b, lgb_b)], axis=1)
    top = row < H
    for i, (la, lb) in enumerate(((lgf_a, lgf_b), (lgb_a, lgb_b))):
        g = jnp.where(top, jnp.exp(la * C), jnp.exp(lb * C))
        dec_scr[i] = jnp.concatenate([g, g], axis=1)
    lo = col < H

    pad_rows = C - N_META
    for i, ref in enumerate((q_ref, k_ref)):
        pad_scr[i, 0:pad_rows, :] = jnp.zeros((pad_rows, LANES), BF16)
        pad_scr[i, pad_rows:C, :] = ref[seq:seq + N_META, :]
    for i, ref in enumerate((v_ref, sg_ref)):
        padw_scr[i, 0:pad_rows, :] = jnp.zeros((pad_rows, W), BF16)
        padw_scr[i, pad_rows:C, :] = ref[seq:seq + N_META, :]

    def seq_rows(c):
        return pl.ds(c * C, C)

    def kv(kc, vc):
        kf = kc.astype(F32)
        ks = jnp.concatenate([kf, kf], axis=1) * tab_scr[1]
        kt = jnp.concatenate([ks[:, :LANES].T, ks[:, LANES:].T], axis=0).astype(BF16)
        x = _dot(kt, vc)
        z = jnp.zeros((H, LANES), F32)
        rows = []
        for d in range(2):
            rows.append(jnp.concatenate([x[d * C:d * C + H, :LANES], z], axis=1))
            rows.append(jnp.concatenate([z, x[d * C + H:(d + 1) * C, LANES:]], axis=1))
        return jnp.concatenate(rows, axis=0)

    kv_scr[0] = kv(pad_scr[1], padw_scr[0])
    for c in range(n_chunks):
        r = seq_rows(c)
        kv_scr[c + 1] = kv(k_ref[r, :], v_ref[r, :])

    s_scr[...] = jnp.zeros((2, C, W), F32)
    for t in range(n_chunks + 1):
        u = n_chunks - t
        sf = s_scr[0]
        sb = s_scr[1]
        st_scr[t, 0:C, :] = sf.astype(BF16)
        st_scr[u, C:2 * C, :] = sb.astype(BF16)
        s_scr[0] = sf * dec_scr[0] + kv_scr[t, 0:C, :]
        s_scr[1] = sb * dec_scr[1] + kv_scr[u, C:2 * C, :]

    def emit(qc, kc, vc, sgc, st):
        qf = qc.astype(F32)
        kf = kc.astype(F32)
        k2 = jnp.concatenate([jnp.where(lo, kf, 0.0), jnp.where(lo, 0.0, kf)], axis=0).astype(BF16)
        p = (_dot_nt(qc, k2) * tab_scr[2]).astype(BF16)
        zb = jnp.zeros((C, LANES), BF16)
        vbd = jnp.concatenate([jnp.concatenate([vc[:, :LANES], zb], axis=1),
                               jnp.concatenate([zb, vc[:, LANES:]], axis=1)], axis=0)
        qs = (jnp.concatenate([qf, qf], axis=1) * tab_scr[0]).astype(BF16)
        o = _dot(p, vbd) + _dot(qs, st)
        halves = []
        for j in range(2):
            oh = o[:, j * LANES:(j + 1) * LANES]
            mu = jnp.mean(oh, axis=-1, keepdims=True)
            var = jnp.mean(jnp.square(oh - mu), axis=-1, keepdims=True)
            halves.append((oh - mu) * lax.rsqrt(var + EPS))
        return (jnp.concatenate(halves, axis=1) * sgc.astype(F32)).astype(BF16)

    for c in range(n_chunks):
        r = seq_rows(c)
        o_ref[r, :] = emit(q_ref[r, :], k_ref[r, :], v_ref[r, :], sg_ref[r, :], st_scr[c + 1])
    om = emit(pad_scr[0], pad_scr[1], padw_scr[0], padw_scr[1], st_scr[0])
    o_ref[seq:seq + N_META, :] = om[pad_rows:, :]


def _retention(mix, rd_lanes, batch, ltok, cast_weights, layer):
    n_chunks = (ltok - N_META) // CHUNK
    pp = RET_PAIRS_PER_STEP
    steps = RET_HEADS // 2 // pp
    narrow, wide = pp * LANES, 2 * pp * LANES
    blk = lambda cols, base: pl.BlockSpec((ltok, cols), lambda b, j: (b, base + j))
    c_in, c_out, c_shapes = _cast_riders(cast_weights, layer, batch * steps, lambda b, j: b * steps + j)
    per_pair = lambda *shape: pltpu.VMEM((pp,) + shape[:-1], shape[-1])
    return pl.pallas_call(
        functools.partial(_ret_kernel, n_chunks=n_chunks, n_cast=len(cast_weights)),
        grid=(batch, steps),
        in_specs=[
            pl.BlockSpec((pp, 8, LANES), lambda b, j: (j, 0, 0)),
            blk(narrow, OFF_QR // narrow),
            blk(narrow, OFF_KR // narrow),
            blk(wide, OFF_VR // wide),
            blk(wide, OFF_GR // wide),
        ] + c_in,
        out_specs=[pl.BlockSpec((ltok, wide), lambda b, j: (b, j))] + c_out,
        out_shape=[jax.ShapeDtypeStruct((batch * ltok, RET_V), BF16)] + c_shapes,
        scratch_shapes=[
            per_pair(3, CHUNK, 2 * LANES, F32),
            per_pair(2, CHUNK, 2 * LANES, F32),
            per_pair(2, CHUNK, LANES, BF16),
            per_pair(2, CHUNK, 2 * LANES, BF16),
            per_pair(n_chunks + 1, 2 * CHUNK, 2 * LANES, F32),
            per_pair(n_chunks + 1, 2 * CHUNK, 2 * LANES, BF16),
            per_pair(2, CHUNK, 2 * LANES, F32),
        ],
        compiler_params=pltpu.CompilerParams(dimension_semantics=("arbitrary", "arbitrary"),
                                             vmem_limit_bytes=VMEM_LIMIT),
        name="retention",
    )(rd_lanes, mix, mix, mix, mix, *cast_weights)


def _decay_lanes(ret_decay_l):
    rd = ret_decay_l.astype(F32)
    pairs = RET_HEADS // 2
    fa, fb = rd[0, 0::2], rd[0, 1::2]
    ba, bb = rd[1, 0::2], rd[1, 1::2]
    full = lambda v: jnp.broadcast_to(v[:, None], (pairs, LANES))
    split = lambda x, y: jnp.concatenate([jnp.broadcast_to(x[:, None], (pairs, RET_QK_DIM)),
                                          jnp.broadcast_to(y[:, None], (pairs, RET_QK_DIM))], axis=1)
    rows = [full(fa), full(fb), full(ba), full(bb), split(fa, fb), split(ba, bb), full(fa), full(fa)]
    return jnp.stack(rows, axis=1)


BAND = 3 * CHUNK
KEYS = BAND + CHUNK


MASK_UPPER, MASK_LOWER, MASK_META, MASK_METAQ, MASK_NONE = range(5)
MASK_ALL = None
N_MASKS = 5
SOFTMAX_ROWS = 32


def _att_kernel(*refs, n_blocks, n_cast):
    (sink_ref, q_ref, k_ref, v_ref), cast_in, o_ref, cast_out, scratch = _split_refs(refs, 4, n_cast)
    k_scr, v_scr, bias_scr, sa_scr, sb_scr, e_scr, t_scr = scratch
    _run_cast_riders(cast_in, cast_out)
    C = CHUNK
    seq = n_blocks * C
    g = pl.program_id(1)
    ltok = seq + N_META
    lane = lax.broadcasted_iota(jnp.int32, (ltok, LANES), 1)
    lo = lane < ATT_HEAD_DIM
    first = g == 0

    def own_head_low(ref):
        x = ref[...].astype(F32)
        return jnp.where(first, x, pltpu.roll(x, ATT_HEAD_DIM, 1))

    kk = own_head_low(k_ref)
    kdup = jnp.where(lo, kk, pltpu.roll(kk, ATT_HEAD_DIM, 1)).astype(BF16)
    vone = jnp.where(lo, own_head_low(v_ref), 1.0).astype(BF16)
    pad = jnp.zeros((C - N_META, LANES), BF16)
    for scr, val in ((k_scr, kdup), (v_scr, vone)):
        meta_tile = jnp.concatenate([val[seq:ltok, :], pad], axis=0)
        for j in range(n_blocks):
            t = j + j // 2
            scr[t * C:(t + 1) * C, :] = val[j * C:(j + 1) * C, :]
            if j % 2 == 1:
                scr[(t + 1) * C:(t + 2) * C, :] = meta_tile

    row = lax.broadcasted_iota(jnp.int32, (C, LANES), 0)
    col = lax.broadcasted_iota(jnp.int32, (C, LANES), 1)
    visible = lambda cond: jnp.where(cond, 0.0, NEG_INF)
    bias_scr[MASK_UPPER] = visible(col >= row)
    bias_scr[MASK_LOWER] = visible(col <= row)
    bias_scr[MASK_META] = visible(col < N_META)
    bias_scr[MASK_METAQ] = visible(col <= row + (C - N_META))
    bias_scr[MASK_NONE] = visible(col < 0)

    def scores(q4, slab_start, s_scr, slot):
        R = q4.shape[0]
        qf = q4.astype(F32)
        lo_r = lax.broadcasted_iota(jnp.int32, (R, LANES), 1) < ATT_HEAD_DIM
        stacked = []
        for p in range(ATT_GROUP // 2):
            qp = qf[:, p * LANES:(p + 1) * LANES]
            stacked += [jnp.where(lo_r, qp, 0.0), jnp.where(lo_r, 0.0, qp)]
        qs = jnp.concatenate(stacked, axis=0).astype(BF16)
        s_scr[slot, 0:ATT_GROUP * R, :] = _dot_nt(qs, k_scr[pl.ds(slab_start, KEYS), :])

    def finish(R, slab_start, masks, s_scr, slot):
        eslot = slot if s_scr is sa_scr else 2 + slot
        SR = min(R, SOFTMAX_ROWS)
        lo_r = lax.broadcasted_iota(jnp.int32, (R, LANES), 1) < ATT_HEAD_DIM
        slab = pl.ds(slab_start, KEYS)
        for h in range(ATT_GROUP):
            sink = sink_ref[g * ATT_GROUP + h]
            for i in range(R // SR):
                rows = pl.ds(h * R + i * SR, SR)
                tiles = []
                for t, mask in enumerate(masks):
                    x = s_scr[slot, rows, t * LANES:(t + 1) * LANES]
                    if mask is not MASK_ALL:
                        x = x + bias_scr[mask, i * SR:(i + 1) * SR, :]
                    tiles.append(x)
                top = jnp.maximum(jnp.maximum(tiles[0], tiles[1]), jnp.maximum(tiles[2], tiles[3]))
                m = jnp.maximum(jnp.max(top, axis=-1, keepdims=True), sink)
                for t in range(len(masks)):
                    e_scr[eslot, rows, t * LANES:(t + 1) * LANES] = jnp.exp(tiles[t] - m).astype(BF16)
                t_scr[eslot, rows, :] = jnp.broadcast_to(jnp.exp(sink - m), (SR, LANES))
        res = _dot(e_scr[eslot, 0:ATT_GROUP * R, :], v_scr[slab, :])
        outs = []
        for p in range(ATT_GROUP // 2):
            ev, od = pl.ds(2 * p * R, R), pl.ds((2 * p + 1) * R, R)
            r_ev, r_od = res[2 * p * R:(2 * p + 1) * R, :], res[(2 * p + 1) * R:(2 * p + 2) * R, :]
            num = jnp.where(lo_r, r_ev, pltpu.roll(r_od, ATT_HEAD_DIM, 1))
            den = (jnp.where(lo_r, pltpu.roll(r_ev, ATT_HEAD_DIM, 1), r_od)
                   + jnp.where(lo_r, t_scr[eslot, ev, :], t_scr[eslot, od, :]))
            outs.append(num / den)
        return jnp.concatenate(outs, axis=1).astype(BF16)

    odd_masks = (MASK_UPPER, MASK_ALL, MASK_META, MASK_LOWER)
    even_masks = (MASK_UPPER, MASK_META, MASK_ALL, MASK_LOWER)
    n_pairs = n_blocks // 2 - 1

    def q_rows(n):
        return pl.ds(pl.multiple_of(n * C, C), C)

    def pair_scores(i, s_scr):
        scores(q_ref[q_rows(2 * i + 1), :], pl.multiple_of(3 * i * C, C), s_scr, 0)
        scores(q_ref[q_rows(2 * i + 2), :], pl.multiple_of((3 * i + 1) * C, C), s_scr, 1)

    def pair_finish(i, s_scr):
        o_ref[q_rows(2 * i + 1), :] = finish(C, pl.multiple_of(3 * i * C, C), odd_masks, s_scr, 0)
        o_ref[q_rows(2 * i + 2), :] = finish(C, pl.multiple_of((3 * i + 1) * C, C), even_masks, s_scr, 1)

    pair_scores(0, sa_scr)
    scores(q_ref[0:C, :], 0, sb_scr, 0)
    o_ref[0:C, :] = finish(C, 0, (MASK_ALL, MASK_LOWER, MASK_META, MASK_NONE), sb_scr, 0)

    def body(j, carry):
        pair_scores(2 * j + 1, sb_scr)
        pair_finish(2 * j, sa_scr)
        pair_scores(2 * j + 2, sa_scr)
        pair_finish(2 * j + 1, sb_scr)
        return carry

    lax.fori_loop(0, (n_pairs - 1) // 2, body, 0)
    last = n_blocks - 1
    last_slab = ((last - 2) + (last - 2) // 2) * C
    scores(q_ref[last * C:seq, :], last_slab, sb_scr, 0)
    scores(q_ref[seq:ltok, :], 0, sb_scr, 1)
    pair_finish(n_pairs - 1, sa_scr)
    o_ref[last * C:seq, :] = finish(C, last_slab, (MASK_NONE, MASK_META, MASK_UPPER, MASK_ALL), sb_scr, 0)
    o_ref[seq:ltok, :] = finish(N_META, 0, (MASK_METAQ, MASK_NONE, MASK_META, MASK_NONE), sb_scr, 1)


def _attention(mix, sink, batch, ltok, cast_weights, layer):
    n_blocks = (ltok - N_META) // CHUNK
    assert n_blocks % 4 == 0
    slab_rows = (n_blocks + n_blocks // 2) * CHUNK
    stacked = ATT_GROUP * CHUNK
    c_in, c_out, c_shapes = _cast_riders(cast_weights, layer, batch * ATT_KV_HEADS,
                                         lambda b, g: b * ATT_KV_HEADS + g)
    return pl.pallas_call(
        functools.partial(_att_kernel, n_blocks=n_blocks, n_cast=len(cast_weights)),
        grid=(batch, ATT_KV_HEADS),
        in_specs=[
            pl.BlockSpec(memory_space=pltpu.SMEM),
            pl.BlockSpec((ltok, 2 * LANES), lambda b, g: (b, OFF_QA // (2 * LANES) + g)),
            pl.BlockSpec((ltok, LANES), lambda b, g: (b, OFF_KA // LANES)),
            pl.BlockSpec((ltok, LANES), lambda b, g: (b, OFF_VA // LANES)),
        ] + c_in,
        out_specs=[pl.BlockSpec((ltok, 2 * LANES), lambda b, g: (b, g))] + c_out,
        out_shape=[jax.ShapeDtypeStruct((batch * ltok, ATT_Q), BF16)] + c_shapes,
        scratch_shapes=[
            pltpu.VMEM((slab_rows, LANES), BF16),
            pltpu.VMEM((slab_rows, LANES), BF16),
            pltpu.VMEM((N_MASKS, CHUNK, LANES), F32),
            pltpu.VMEM((2, stacked, KEYS), F32),
            pltpu.VMEM((2, stacked, KEYS), F32),
            pltpu.VMEM((4, stacked, KEYS), BF16),
            pltpu.VMEM((4, stacked, LANES), F32),
        ],
        compiler_params=pltpu.CompilerParams(dimension_semantics=("arbitrary", "arbitrary"),
                                             vmem_limit_bytes=VMEM_LIMIT),
        name="attention",
    )(sink, mix, mix, mix, *cast_weights)


POST_TM = 688
POST_SUB = ((0, 416), (416, 272))
FF_CHUNK = 1024


def _post_ffn_kernel(h_ref, yr_ref, ya_ref, gt_ref, wr_ref, wa_ref, wm_ref, w1_ref, w2_ref,
                     gpost_ref, gpre_ref, gffpost_ref, o_ref):
    def chain(r):
        y_r = _dot(yr_ref[r, :], wr_ref[...])
        y_a = _dot(ya_ref[r, :], wa_ref[...])
        z = gt_ref[r, :D_MODEL].astype(F32) * y_r + gt_ref[r, D_MODEL:].astype(F32) * y_a
        yield
        mix = _dot(z.astype(BF16), wm_ref[...])
        yield
        h1 = h_ref[r, :] + _rms(mix, gpost_ref[...])
        u = _rms(h1, gpre_ref[...]).astype(BF16)
        yield
        ff = jnp.zeros(h1.shape, F32)
        for c0 in range(0, D_FF, FF_CHUNK):
            hid = jnp.maximum(_dot(u, w1_ref[:, c0:c0 + FF_CHUNK]), 0.0)
            ff = ff + _dot((hid * hid).astype(BF16), w2_ref[c0:c0 + FF_CHUNK, :])
            yield
        o_ref[r, :] = h1 + _rms(ff, gffpost_ref[...])

    chains = [chain(pl.ds(start, rows)) for start, rows in POST_SUB]
    while chains:
        chains = [c for c in chains if next(c, "done") != "done"]


def _post_ffn(h, y_r, y_a, gates, wr, wa, wm, w1, w2, gpost, gpre, gffpost, layer, batch, ltok, out_rows):
    tiles = ltok // POST_TM
    tok = lambda cols: pl.BlockSpec((POST_TM, cols), lambda b, i: (b * tiles + i, 0))
    gain = _layer_resident((1, D_MODEL), layer)
    return pl.pallas_call(
        _post_ffn_kernel,
        grid=(batch, tiles),
        in_specs=[tok(D_MODEL), tok(RET_V), tok(ATT_Q), tok(GATE_COLS),
                  _resident((RET_V, D_MODEL)), _resident((ATT_Q, D_MODEL)), _resident((D_MODEL, D_MODEL)),
                  _resident((D_MODEL, D_FF)), _resident((D_FF, D_MODEL)), gain, gain, gain],
        out_specs=pl.BlockSpec((None, POST_TM, D_MODEL), lambda b, i: (b, i, 0)),
        out_shape=jax.ShapeDtypeStruct((batch, out_rows, D_MODEL), F32),
        compiler_params=pltpu.CompilerParams(dimension_semantics=("arbitrary", "arbitrary"),
                                             vmem_limit_bytes=VMEM_LIMIT),
        name="post_ffn",
    )(h, y_r, y_a, gates, wr, wa, wm, w1, w2, gpost, gpre, gffpost)


def kernel(x, meta_tokens, w_in, w_ret_o, w_att_o, w_mix_o, w_ff1, w_ff2,
           norm_mix_pre, norm_mix_post, norm_ff_pre, norm_ff_post, ret_decay, attn_sink):
    batch, seq, d = x.shape
    depth = w_in.shape[0]
    ltok = seq + N_META
    assert d == D_MODEL and seq % (2 * CHUNK) == 0 and seq >= BAND and ltok % IN_TM == 0 and ltok % POST_TM == 0
    meta = jnp.broadcast_to(meta_tokens[None].astype(x.dtype), (batch, N_META, d))
    h = jnp.concatenate([x, meta], axis=1).reshape(batch * ltok, d)
    tables = _rotary_tables(seq)
    gains = [g.astype(F32).reshape(depth, 1, d) for g in (norm_mix_pre, norm_mix_post, norm_ff_pre, norm_ff_post)]
    post_weights = tuple(w.astype(F32) for w in (w_ret_o, w_att_o, w_mix_o, w_ff1, w_ff2))
    w_in_f = w_in.astype(F32)
    w_in_b = w_in_f[0].astype(BF16)
    for l in range(depth):
        last = l == depth - 1
        mix, gates = _in_proj(h, gains[0], w_in_b, tables, l, ltok // IN_TM)
        y_r, *post_b = _retention(mix, _decay_lanes(ret_decay[l]), batch, ltok, post_weights, l)
        y_a, *next_in = _attention(mix, attn_sink[l].astype(F32), batch, ltok, () if last else (w_in_f,), l + 1)
        if not last:
            w_in_b, = next_in
        h = _post_ffn(h, y_r, y_a, gates, *post_b,
                      gains[1], gains[2], gains[3], l, batch, ltok, seq if last else ltok)
        if not last:
            h = h.reshape(batch * ltok, d)
    return h
```

```python
import functools

import jax
import jax.numpy as jnp
from jax import lax
from jax.experimental import pallas as pl
from jax.experimental.pallas import tpu as pltpu

F32 = jnp.float32
BF16 = jnp.bfloat16

D_MODEL = 1024
N_META = 16
CHUNK = 128
RET_HEADS = 8
RET_QK_DIM = 64
RET_V_DIM = 128
ATT_Q_HEADS = 8
ATT_KV_HEADS = 2
ATT_GROUP = ATT_Q_HEADS // ATT_KV_HEADS
ATT_HEAD_DIM = 64
ROPE_DIM = 16
ROPE_THETA = 500000.0
XPOS_THETA = 10000.0
D_FF = 4 * D_MODEL
EPS = 1e-6
NEG_INF = -1e30

RET_QK = RET_HEADS * RET_QK_DIM
RET_V = RET_HEADS * RET_V_DIM
ATT_Q = ATT_Q_HEADS * ATT_HEAD_DIM
ATT_KV = ATT_KV_HEADS * ATT_HEAD_DIM
OFF_QR = 0
OFF_KR = OFF_QR + RET_QK
OFF_VR = OFF_KR + RET_QK
OFF_GR = OFF_VR + RET_V
OFF_QA = OFF_GR + RET_V
OFF_KA = OFF_QA + ATT_Q
OFF_VA = OFF_KA + ATT_KV
MIX_COLS = OFF_VA + ATT_KV
GATE_COLS = 2 * D_MODEL
D_IN = MIX_COLS + GATE_COLS

LANES = 128
MXU_N = 256
VMEM_LIMIT = 60 * 1024 * 1024


def _dot(a, b):
    return jnp.dot(a, b, preferred_element_type=F32)


def _dot_nt(a, b):
    return lax.dot_general(a, b, (((1,), (1,)), ((), ())), preferred_element_type=F32)


def _rms(x, g):
    return x * lax.rsqrt(jnp.mean(x * x, axis=-1, keepdims=True) + EPS) * g


def _sigmoid(x):
    return 1.0 / (1.0 + jnp.exp(-x))


def _resident(shape):
    return pl.BlockSpec(shape, lambda *_: (0,) * len(shape), pipeline_mode=pl.Buffered(1))


def _cast_riders(weights, layer, steps, step_index):
    in_specs, out_specs, out_shapes = [], [], []
    for w in weights:
        _, k, n = w.shape
        rows = k // steps
        assert rows * steps == k and rows % 16 == 0, (k, steps)
        in_specs.append(pl.BlockSpec((None, rows, n), lambda *g: (layer, step_index(*g), 0)))
        out_specs.append(pl.BlockSpec((rows, n), lambda *g: (step_index(*g), 0)))
        out_shapes.append(jax.ShapeDtypeStruct((k, n), BF16))
    return in_specs, out_specs, out_shapes


def _split_refs(refs, n_in, n_cast):
    a, b = n_in + n_cast, n_in + n_cast + 1
    return refs[:n_in], refs[n_in:a], refs[a], refs[b:b + n_cast], refs[b + n_cast:]


def _run_cast_riders(cast_in, cast_out):
    for src, dst in zip(cast_in, cast_out):
        dst[...] = src[...].astype(BF16)


def _layer_resident(shape, layer):
    return pl.BlockSpec((None,) + shape, lambda *_: (layer,) + (0,) * len(shape), pipeline_mode=pl.Buffered(1))


def _rotary_tables(seq):
    pos = jnp.concatenate([jnp.arange(N_META, N_META + seq), jnp.arange(N_META)]).astype(F32)
    d = jnp.arange(LANES) % RET_QK_DIM

    def table(theta, rot_dim):
        half = rot_dim // 2
        freqs = jnp.power(jnp.float32(theta), -jnp.arange(0, rot_dim, 2, dtype=F32) / rot_dim)
        ang = pos[:, None] * freqs[None, :]
        cos, sin = jnp.cos(ang), jnp.sin(ang)
        idx = d % half
        rotated = (d < rot_dim)[None, :]
        cos_l = jnp.where(rotated, cos[:, idx], 1.0)
        sin_l = jnp.where(rotated, jnp.where((d < half)[None, :], -sin[:, idx], sin[:, idx]), 0.0)
        return cos_l, sin_l

    cr, sr = table(XPOS_THETA, RET_QK_DIM)
    ca, sa = table(ROPE_THETA, ROPE_DIM)
    qs_r = RET_QK_DIM ** -0.5
    qs_a = ATT_HEAD_DIM ** -0.5
    return (cr * qs_r, sr * qs_r, cr, sr, ca * qs_a, sa * qs_a, ca, sa)


def _rotate(x, cos, sin, half):
    lane = lax.broadcasted_iota(jnp.int32, x.shape, 1) % RET_QK_DIM
    partner = jnp.where(lane < half, pltpu.roll(x, LANES - half, 1), pltpu.roll(x, half, 1))
    return x * cos + partner * sin


IN_TM = 688
IN_SUB = ((0, 272), (272, 416))


def _in_proj_kernel(h_ref, g_ref, w_ref, crq, srq, crk, srk, caq, saq, cak, sak, mix_ref, gate_ref):
    def chain(r):
        u = _rms(h_ref[r, :], g_ref[...]).astype(BF16)
        yield

        def proj(c0, n):
            return _dot(u, w_ref[:, c0:c0 + n])

        def rot_store(c0, cos_ref, sin_ref, half):
            p = proj(c0, MXU_N)
            for j in range(MXU_N // LANES):
                t = _rotate(p[:, j * LANES:(j + 1) * LANES], cos_ref[r, :], sin_ref[r, :], half)
                mix_ref[r, c0 + j * LANES:c0 + (j + 1) * LANES] = t.astype(BF16)

        for c0 in range(OFF_QR, OFF_KR, MXU_N):
            rot_store(c0, crq, srq, RET_QK_DIM // 2)
            yield
        for c0 in range(OFF_KR, OFF_VR, MXU_N):
            rot_store(c0, crk, srk, RET_QK_DIM // 2)
            yield
        for c0 in range(OFF_GR, OFF_QA, MXU_N):
            p = proj(c0, MXU_N)
            mix_ref[r, c0:c0 + MXU_N] = (p * _sigmoid(p)).astype(BF16)
            yield
        for c0 in range(OFF_QA, OFF_KA, MXU_N):
            rot_store(c0, caq, saq, ROPE_DIM // 2)
            yield
        p = proj(OFF_KA, MXU_N)
        mix_ref[r, OFF_KA:OFF_VA] = _rotate(p[:, :LANES], cak[r, :], sak[r, :], ROPE_DIM // 2).astype(BF16)
        mix_ref[r, OFF_VA:MIX_COLS] = p[:, LANES:].astype(BF16)
        yield
        for c0 in range(0, GATE_COLS, MXU_N):
            gate_ref[r, c0:c0 + MXU_N] = _sigmoid(proj(MIX_COLS + c0, MXU_N)).astype(BF16)
            yield
        for c0 in range(OFF_VR, OFF_GR, MXU_N):
            mix_ref[r, c0:c0 + MXU_N] = proj(c0, MXU_N).astype(BF16)
            yield

    chains = [chain(pl.ds(start, rows)) for start, rows in IN_SUB]
    while chains:
        chains = [c for c in chains if next(c, "done") != "done"]


def _in_proj(h, gains, w, tables, layer, tiles_per_row):
    t = h.shape[0]
    tok = lambda cols: pl.BlockSpec((IN_TM, cols), lambda i: (i, 0))
    tab = pl.BlockSpec((IN_TM, LANES), lambda i: (i % tiles_per_row, 0))
    return pl.pallas_call(
        _in_proj_kernel,
        grid=(t // IN_TM,),
        in_specs=[tok(D_MODEL), _layer_resident((1, D_MODEL), layer), _resident((D_MODEL, D_IN))]
                 + [tab] * 8,
        out_specs=[tok(MIX_COLS), tok(GATE_COLS)],
        out_shape=[jax.ShapeDtypeStruct((t, MIX_COLS), BF16), jax.ShapeDtypeStruct((t, GATE_COLS), BF16)],
        compiler_params=pltpu.CompilerParams(dimension_semantics=("arbitrary",), vmem_limit_bytes=VMEM_LIMIT),
        name="in_proj",
    )(h, gains, w, *tables)


RET_PAIRS_PER_STEP = 2


def _ret_kernel(*refs, n_chunks, n_cast):
    (rd_ref, q_ref, k_ref, v_ref, sg_ref), cast_in, o_ref, cast_out, scratch = _split_refs(refs, 5, n_cast)
    _run_cast_riders(cast_in, cast_out)
    for p in range(RET_PAIRS_PER_STEP):
        narrow, wide = pl.ds(p * LANES, LANES), pl.ds(2 * p * LANES, 2 * LANES)
        _ret_pair(rd_ref.at[p], q_ref.at[:, narrow], k_ref.at[:, narrow], v_ref.at[:, wide], sg_ref.at[:, wide],
                  o_ref.at[:, wide], [s.at[p] for s in scratch], n_chunks)


def _ret_pair(rd_ref, q_ref, k_ref, v_ref, sg_ref, o_ref, scratch, n_chunks):
    tab_scr, dec_scr, pad_scr, padw_scr, kv_scr, st_scr, s_scr = scratch
    C = CHUNK
    seq = n_chunks * C
    W = 2 * LANES
    H = RET_QK_DIM
    lg = -jnp.exp(rd_ref[...])
    lgf_a, lgf_b, lgb_a, lgb_b, lgf_p, lgb_p = [lg[i:i + 1, :] for i in range(6)]

    row = lax.broadcasted_iota(jnp.int32, (C, LANES), 0)
    col = lax.broadcasted_iota(jnp.int32, (C, LANES), 1)
    a = row.astype(F32)
    rel = (row - col).astype(F32)

    def decay_matrix(lf, lb):
        return jnp.where(rel >= 0, jnp.exp(lf * jnp.maximum(rel, 0.0)), jnp.exp(lb * jnp.maximum(-rel, 0.0)))

    tab_scr[0] = jnp.concatenate([jnp.exp(lgf_p * (a + 1.0)), jnp.exp(lgb_p * (C - a))], axis=1)
    tab_scr[1] = jnp.concatenate([jnp.exp(lgf_p * (C - 1.0 - a)), jnp.exp(lgb_p * a)], axis=1)
    tab_scr[2] = jnp.concatenate([decay_matrix(lgf_a, lgb_a), decay_matrix(lgf_b, lgb_b)], axis=1)
    top = row < H
    for i, (la, lb) in enumerate(((lgf_a, lgf_b), (lgb_a, lgb_b))):
        g = jnp.where(top, jnp.exp(la * C), jnp.exp(lb * C))
        dec_scr[i] = jnp.concatenate([g, g], axis=1)
    lo = col < H

    pad_rows = C - N_META
    for i, ref in enumerate((q_ref, k_ref)):
        pad_scr[i, 0:pad_rows, :] = jnp.zeros((pad_rows, LANES), BF16)
        pad_scr[i, pad_rows:C, :] = ref[seq:seq + N_META, :]
    for i, ref in enumerate((v_ref, sg_ref)):
        padw_scr[i, 0:pad_rows, :] = jnp.zeros((pad_rows, W), BF16)
        padw_scr[i, pad_rows:C, :] = ref[seq:seq + N_META, :]

    def seq_rows(c):
        return pl.ds(c * C, C)

    def kv(kc, vc):
        kf = kc.astype(F32)
        ks = jnp.concatenate([kf, kf], axis=1) * tab_scr[1]
        kt = jnp.concatenate([ks[:, :LANES].T, ks[:, LANES:].T], axis=0).astype(BF16)
        x = _dot(kt, vc)
        z = jnp.zeros((H, LANES), F32)
        rows = []
        for d in range(2):
            rows.append(jnp.concatenate([x[d * C:d * C + H, :LANES], z], axis=1))
            rows.append(jnp.concatenate([z, x[d * C + H:(d + 1) * C, LANES:]], axis=1))
        return jnp.concatenate(rows, axis=0)

    kv_scr[0] = kv(pad_scr[1], padw_scr[0])
    for c in range(n_chunks):
        r = seq_rows(c)
        kv_scr[c + 1] = kv(k_ref[r, :], v_ref[r, :])

    s_scr[...] = jnp.zeros((2, C, W), F32)
    for t in range(n_chunks + 1):
        u = n_chunks - t
        sf = s_scr[0]
        sb = s_scr[1]
        st_scr[t, 0:C, :] = sf.astype(BF16)
        st_scr[u, C:2 * C, :] = sb.astype(BF16)
        s_scr[0] = sf * dec_scr[0] + kv_scr[t, 0:C, :]
        s_scr[1] = sb * dec_scr[1] + kv_scr[u, C:2 * C, :]

    def emit(qc, kc, vc, sgc, st):
        qf = qc.astype(F32)
        kf = kc.astype(F32)
        k2 = jnp.concatenate([jnp.where(lo, kf, 0.0), jnp.where(lo, 0.0, kf)], axis=0).astype(BF16)
        p = (_dot_nt(qc, k2) * tab_scr[2]).astype(BF16)
        zb = jnp.zeros((C, LANES), BF16)
        vbd = jnp.concatenate([jnp.concatenate([vc[:, :LANES], zb], axis=1),
                               jnp.concatenate([zb, vc[:, LANES:]], axis=1)], axis=0)
        qs = (jnp.concatenate([qf, qf], axis=1) * tab_scr[0]).astype(BF16)
        o = _dot(p, vbd) + _dot(qs, st)
        halves = []
        for j in range(2):
            oh = o[:, j * LANES:(j + 1) * LANES]
            mu = jnp.mean(oh, axis=-1, keepdims=True)
            var = jnp.mean(jnp.square(oh - mu), axis=-1, keepdims=True)
            halves.append((oh - mu) * lax.rsqrt(var + EPS))
        return (jnp.concatenate(halves, axis=1) * sgc.astype(F32)).astype(BF16)

    for c in range(n_chunks):
        r = seq_rows(c)
        o_ref[r, :] = emit(q_ref[r, :], k_ref[r, :], v_ref[r, :], sg_ref[r, :], st_scr[c + 1])
    om = emit(pad_scr[0], pad_scr[1], padw_scr[0], padw_scr[1], st_scr[0])
    o_ref[seq:seq + N_META, :] = om[pad_rows:, :]


def _retention(mix, rd_lanes, batch, ltok, cast_weights, layer):
    n_chunks = (ltok - N_META) // CHUNK
    pp = RET_PAIRS_PER_STEP
    steps = RET_HEADS // 2 // pp
    narrow, wide = pp * LANES, 2 * pp * LANES
    blk = lambda cols, base: pl.BlockSpec((ltok, cols), lambda b, j: (b, base + j))
    c_in, c_out, c_shapes = _cast_riders(cast_weights, layer, batch * steps, lambda b, j: b * steps + j)
    per_pair = lambda *shape: pltpu.VMEM((pp,) + shape[:-1], shape[-1])
    return pl.pallas_call(
        functools.partial(_ret_kernel, n_chunks=n_chunks, n_cast=len(cast_weights)),
        grid=(batch, steps),
        in_specs=[
            pl.BlockSpec((pp, 8, LANES), lambda b, j: (j, 0, 0)),
            blk(narrow, OFF_QR // narrow),
            blk(narrow, OFF_KR // narrow),
            blk(wide, OFF_VR // wide),
            blk(wide, OFF_GR // wide),
        ] + c_in,
        out_specs=[pl.BlockSpec((ltok, wide), lambda b, j: (b, j))] + c_out,
        out_shape=[jax.ShapeDtypeStruct((batch * ltok, RET_V), BF16)] + c_shapes,
        scratch_shapes=[
            per_pair(3, CHUNK, 2 * LANES, F32),
            per_pair(2, CHUNK, 2 * LANES, F32),
            per_pair(2, CHUNK, LANES, BF16),
            per_pair(2, CHUNK, 2 * LANES, BF16),
            per_pair(n_chunks + 1, 2 * CHUNK, 2 * LANES, F32),
            per_pair(n_chunks + 1, 2 * CHUNK, 2 * LANES, BF16),
            per_pair(2, CHUNK, 2 * LANES, F32),
        ],
        compiler_params=pltpu.CompilerParams(dimension_semantics=("arbitrary", "arbitrary"),
                                             vmem_limit_bytes=VMEM_LIMIT),
        name="retention",
    )(rd_lanes, mix, mix, mix, mix, *cast_weights)


def _decay_lanes(ret_decay_l):
    rd = ret_decay_l.astype(F32)
    pairs = RET_HEADS // 2
    fa, fb = rd[0, 0::2], rd[0, 1::2]
    ba, bb = rd[1, 0::2], rd[1, 1::2]
    full = lambda v: jnp.broadcast_to(v[:, None], (pairs, LANES))
    split = lambda x, y: jnp.concatenate([jnp.broadcast_to(x[:, None], (pairs, RET_QK_DIM)),
                                          jnp.broadcast_to(y[:, None], (pairs, RET_QK_DIM))], axis=1)
    rows = [full(fa), full(fb), full(ba), full(bb), split(fa, fb), split(ba, bb), full(fa), full(fa)]
    return jnp.stack(rows, axis=1)


BAND = 3 * CHUNK
KEYS = BAND + CHUNK


MASK_UPPER, MASK_LOWER, MASK_METAQ, MASK_NONE, MASK_META = range(5)
MASK_ALL = None
N_MASKS = MASK_META + ATT_GROUP
SOFTMAX_ROWS = 32


def _att_kernel(*refs, n_blocks, n_cast):
    (sink_ref, q_ref, k_ref, v_ref), cast_in, o_ref, cast_out, scratch = _split_refs(refs, 4, n_cast)
    _run_cast_riders(cast_in, cast_out)
    k_scr, v_scr, *shared = scratch
    for g in range(ATT_KV_HEADS):
        cols = pl.ds(2 * g * LANES, 2 * LANES)
        _att_group(g, sink_ref, q_ref.at[:, cols], k_ref, v_ref, o_ref.at[:, cols],
                   [k_scr.at[g], v_scr.at[g]] + shared, n_blocks)


def _att_group(g, sink_ref, q_ref, k_ref, v_ref, o_ref, scratch, n_blocks):
    k_scr, v_scr, bias_scr, sa_scr, sb_scr, e_scr = scratch
    C = CHUNK
    seq = n_blocks * C
    ltok = seq + N_META
    lane = lax.broadcasted_iota(jnp.int32, (ltok, LANES), 1)
    lo = lane < ATT_HEAD_DIM

    def own_head_low(ref):
        x = ref[...].astype(F32)
        return x if g == 0 else pltpu.roll(x, ATT_HEAD_DIM, 1)

    kk = own_head_low(k_ref)
    kdup = jnp.where(lo, kk, pltpu.roll(kk, ATT_HEAD_DIM, 1)).astype(BF16)
    vone = jnp.where(lo, own_head_low(v_ref), 1.0).astype(BF16)
    tail_row = lax.broadcasted_iota(jnp.int32, (C - N_META, LANES), 0)
    tail_lane = lax.broadcasted_iota(jnp.int32, (C - N_META, LANES), 1)
    sink_value = jnp.where((tail_row == 0) & (tail_lane >= ATT_HEAD_DIM), 1.0, 0.0).astype(BF16)
    for scr, val, tail in ((k_scr, kdup, jnp.zeros((C - N_META, LANES), BF16)), (v_scr, vone, sink_value)):
        meta_tile = jnp.concatenate([val[seq:ltok, :], tail], axis=0)
        for j in range(n_blocks):
            t = j + j // 2
            scr[t * C:(t + 1) * C, :] = val[j * C:(j + 1) * C, :]
            if j % 2 == 1:
                scr[(t + 1) * C:(t + 2) * C, :] = meta_tile

    row = lax.broadcasted_iota(jnp.int32, (C, LANES), 0)
    col = lax.broadcasted_iota(jnp.int32, (C, LANES), 1)
    visible = lambda cond: jnp.where(cond, 0.0, NEG_INF)
    bias_scr[MASK_UPPER] = visible(col >= row)
    bias_scr[MASK_LOWER] = visible(col <= row)
    bias_scr[MASK_METAQ] = visible(col <= row + (C - N_META))
    bias_scr[MASK_NONE] = visible(col < 0)
    for h in range(ATT_GROUP):
        sink = sink_ref[g * ATT_GROUP + h]
        bias_scr[MASK_META + h] = jnp.where(col < N_META, 0.0, jnp.where(col == N_META, sink, NEG_INF))

    def scores(q4, slab_start, masks, s_scr, slot):
        R = q4.shape[0]
        qf = q4.astype(F32)
        lo_r = lax.broadcasted_iota(jnp.int32, (R, LANES), 1) < ATT_HEAD_DIM
        stacked = []
        for p in range(ATT_GROUP // 2):
            qp = qf[:, p * LANES:(p + 1) * LANES]
            stacked += [jnp.where(lo_r, qp, 0.0), jnp.where(lo_r, 0.0, qp)]
        qs = jnp.concatenate(stacked, axis=0).astype(BF16)

        def bias_tile(h, mask):
            if mask is MASK_ALL:
                return jnp.zeros((R, LANES), F32)
            return bias_scr[MASK_META + h if mask == MASK_META else mask, 0:R, :]

        bias = jnp.concatenate([jnp.concatenate([bias_tile(h, mask) for mask in masks], axis=1)
                                for h in range(ATT_GROUP)], axis=0)
        s_scr[slot, 0:ATT_GROUP * R, :] = _dot_nt(qs, k_scr[pl.ds(slab_start, KEYS), :]) + bias

    def finish(R, slab_start, s_scr, slot):
        eslot = slot if s_scr is sa_scr else 2 + slot
        SR = min(R, SOFTMAX_ROWS)
        lo_r = lax.broadcasted_iota(jnp.int32, (R, LANES), 1) < ATT_HEAD_DIM
        slab = pl.ds(slab_start, KEYS)
        for r0 in range(0, ATT_GROUP * R, SR):
            rows = pl.ds(r0, SR)
            tiles = [s_scr[slot, rows, t * LANES:(t + 1) * LANES] for t in range(KEYS // LANES)]
            top = jnp.maximum(jnp.maximum(tiles[0], tiles[1]), jnp.maximum(tiles[2], tiles[3]))
            m = jnp.max(top, axis=-1, keepdims=True)
            e_scr[eslot, rows, :] = jnp.exp((s_scr[slot, rows, :] - m).astype(BF16))
        res = _dot(e_scr[eslot, 0:ATT_GROUP * R, :], v_scr[slab, :])
        outs = []
        for p in range(ATT_GROUP // 2):
            r_ev, r_od = res[2 * p * R:(2 * p + 1) * R, :], res[(2 * p + 1) * R:(2 * p + 2) * R, :]
            num = jnp.where(lo_r, r_ev, pltpu.roll(r_od, ATT_HEAD_DIM, 1))
            den = jnp.where(lo_r, pltpu.roll(r_ev, ATT_HEAD_DIM, 1), r_od)
            outs.append(num / den)
        return jnp.concatenate(outs, axis=1).astype(BF16)

    odd_masks = (MASK_UPPER, MASK_ALL, MASK_META, MASK_LOWER)
    even_masks = (MASK_UPPER, MASK_META, MASK_ALL, MASK_LOWER)
    n_pairs = n_blocks // 2 - 1

    def q_rows(n):
        return pl.ds(pl.multiple_of(n * C, C), C)

    def pair_scores(i, s_scr):
        scores(q_ref[q_rows(2 * i + 1), :], pl.multiple_of(3 * i * C, C), odd_masks, s_scr, 0)
        scores(q_ref[q_rows(2 * i + 2), :], pl.multiple_of((3 * i + 1) * C, C), even_masks, s_scr, 1)

    def pair_finish(i, s_scr):
        o_ref[q_rows(2 * i + 1), :] = finish(C, pl.multiple_of(3 * i * C, C), s_scr, 0)
        o_ref[q_rows(2 * i + 2), :] = finish(C, pl.multiple_of((3 * i + 1) * C, C), s_scr, 1)

    pair_scores(0, sa_scr)
    scores(q_ref[0:C, :], 0, (MASK_ALL, MASK_LOWER, MASK_META, MASK_NONE), sb_scr, 0)
    o_ref[0:C, :] = finish(C, 0, sb_scr, 0)

    def body(j, carry):
        pair_scores(2 * j + 1, sb_scr)
        pair_finish(2 * j, sa_scr)
        pair_scores(2 * j + 2, sa_scr)
        pair_finish(2 * j + 1, sb_scr)
        return carry

    lax.fori_loop(0, (n_pairs - 1) // 2, body, 0)
    last = n_blocks - 1
    last_slab = ((last - 2) + (last - 2) // 2) * C
    scores(q_ref[last * C:seq, :], last_slab, (MASK_NONE, MASK_META, MASK_UPPER, MASK_ALL), sb_scr, 0)
    scores(q_ref[seq:ltok, :], 0, (MASK_METAQ, MASK_NONE, MASK_META, MASK_NONE), sb_scr, 1)
    pair_finish(n_pairs - 1, sa_scr)
    o_ref[last * C:seq, :] = finish(C, last_slab, sb_scr, 0)
    o_ref[seq:ltok, :] = finish(N_META, 0, sb_scr, 1)


def _attention(mix, sink, batch, ltok, cast_weights, layer):
    n_blocks = (ltok - N_META) // CHUNK
    assert n_blocks % 4 == 0
    slab_rows = (n_blocks + n_blocks // 2) * CHUNK
    stacked = ATT_GROUP * CHUNK
    c_in, c_out, c_shapes = _cast_riders(cast_weights, layer, batch, lambda b: b)
    return pl.pallas_call(
        functools.partial(_att_kernel, n_blocks=n_blocks, n_cast=len(cast_weights)),
        grid=(batch,),
        in_specs=[
            pl.BlockSpec(memory_space=pltpu.SMEM),
            pl.BlockSpec((ltok, ATT_Q), lambda b: (b, OFF_QA // ATT_Q)),
            pl.BlockSpec((ltok, LANES), lambda b: (b, OFF_KA // LANES)),
            pl.BlockSpec((ltok, LANES), lambda b: (b, OFF_VA // LANES)),
        ] + c_in,
        out_specs=[pl.BlockSpec((ltok, ATT_Q), lambda b: (b, 0))] + c_out,
        out_shape=[jax.ShapeDtypeStruct((batch * ltok, ATT_Q), BF16)] + c_shapes,
        scratch_shapes=[
            pltpu.VMEM((ATT_KV_HEADS, slab_rows, LANES), BF16),
            pltpu.VMEM((ATT_KV_HEADS, slab_rows, LANES), BF16),
            pltpu.VMEM((N_MASKS, CHUNK, LANES), F32),
            pltpu.VMEM((2, stacked, KEYS), F32),
            pltpu.VMEM((2, stacked, KEYS), F32),
            pltpu.VMEM((4, stacked, KEYS), BF16),
        ],
        compiler_params=pltpu.CompilerParams(dimension_semantics=("arbitrary",), vmem_limit_bytes=VMEM_LIMIT),
        name="attention",
    )(sink, mix, mix, mix, *cast_weights)


POST_TM = 688
POST_SUB = ((0, 416), (416, 272))
FF_CHUNK = 1024


def _post_ffn_kernel(h_ref, yr_ref, ya_ref, gt_ref, wr_ref, wa_ref, wm_ref, w1_ref, w2_ref,
                     gpost_ref, gpre_ref, gffpost_ref, o_ref):
    def chain(r):
        y_r = _dot(yr_ref[r, :], wr_ref[...])
        y_a = _dot(ya_ref[r, :], wa_ref[...])
        z = gt_ref[r, :D_MODEL].astype(F32) * y_r + gt_ref[r, D_MODEL:].astype(F32) * y_a
        yield
        mix = _dot(z.astype(BF16), wm_ref[...])
        yield
        h1 = h_ref[r, :] + _rms(mix, gpost_ref[...])
        u = _rms(h1, gpre_ref[...]).astype(BF16)
        yield
        ff = jnp.zeros(h1.shape, F32)
        for c0 in range(0, D_FF, FF_CHUNK):
            hid = jnp.maximum(_dot(u, w1_ref[:, c0:c0 + FF_CHUNK]), 0.0)
            ff = ff + _dot((hid * hid).astype(BF16), w2_ref[c0:c0 + FF_CHUNK, :])
            yield
        o_ref[r, :] = h1 + _rms(ff, gffpost_ref[...])

    chains = [chain(pl.ds(start, rows)) for start, rows in POST_SUB]
    while chains:
        chains = [c for c in chains if next(c, "done") != "done"]


def _post_ffn(h, y_r, y_a, gates, wr, wa, wm, w1, w2, gpost, gpre, gffpost, layer, batch, ltok, out_rows):
    tiles = ltok // POST_TM
    tok = lambda cols: pl.BlockSpec((POST_TM, cols), lambda b, i: (b * tiles + i, 0))
    gain = _layer_resident((1, D_MODEL), layer)
    return pl.pallas_call(
        _post_ffn_kernel,
        grid=(batch, tiles),
        in_specs=[tok(D_MODEL), tok(RET_V), tok(ATT_Q), tok(GATE_COLS),
                  _resident((RET_V, D_MODEL)), _resident((ATT_Q, D_MODEL)), _resident((D_MODEL, D_MODEL)),
                  _resident((D_MODEL, D_FF)), _resident((D_FF, D_MODEL)), gain, gain, gain],
        out_specs=pl.BlockSpec((None, POST_TM, D_MODEL), lambda b, i: (b, i, 0)),
        out_shape=jax.ShapeDtypeStruct((batch, out_rows, D_MODEL), F32),
        compiler_params=pltpu.CompilerParams(dimension_semantics=("arbitrary", "arbitrary"),
                                             vmem_limit_bytes=VMEM_LIMIT),
        name="post_ffn",
    )(h, y_r, y_a, gates, wr, wa, wm, w1, w2, gpost, gpre, gffpost)


def kernel(x, meta_tokens, w_in, w_ret_o, w_att_o, w_mix_o, w_ff1, w_ff2,
           norm_mix_pre, norm_mix_post, norm_ff_pre, norm_ff_post, ret_decay, attn_sink):
    batch, seq, d = x.shape
    depth = w_in.shape[0]
    ltok = seq + N_META
    assert d == D_MODEL and seq % (2 * CHUNK) == 0 and seq >= BAND and ltok % IN_TM == 0 and ltok % POST_TM == 0
    meta = jnp.broadcast_to(meta_tokens[None].astype(x.dtype), (batch, N_META, d))
    h = jnp.concatenate([x, meta], axis=1).reshape(batch * ltok, d)
    tables = _rotary_tables(seq)
    gains = [g.astype(F32).reshape(depth, 1, d) for g in (norm_mix_pre, norm_mix_post, norm_ff_pre, norm_ff_post)]
    post_weights = tuple(w.astype(F32) for w in (w_ret_o, w_att_o, w_mix_o, w_ff1, w_ff2))
    w_in_f = w_in.astype(F32)
    w_in_b = w_in_f[0].astype(BF16)
    for l in range(depth):
        last = l == depth - 1
        mix, gates = _in_proj(h, gains[0], w_in_b, tables, l, ltok // IN_TM)
        y_r, *post_b = _retention(mix, _decay_lanes(ret_decay[l]), batch, ltok, post_weights, l)
        y_a, *next_in = _attention(mix, attn_sink[l].astype(F32), batch, ltok, () if last else (w_in_f,), l + 1)
        if not last:
            w_in_b, = next_in
        h = _post_ffn(h, y_r, y_a, gates, *post_b,
                      gains[1], gains[2], gains[3], l, batch, ltok, seq if last else ltok)
        if not last:
            h = h.reshape(batch * ltok, d)
    return h
```

```python
import functools

import jax
import jax.numpy as jnp
from jax import lax
from jax.experimental import pallas as pl
from jax.experimental.pallas import tpu as pltpu

F32 = jnp.float32
BF16 = jnp.bfloat16

D_MODEL = 1024
N_META = 16
CHUNK = 128
RET_HEADS = 8
RET_QK_DIM = 64
RET_V_DIM = 128
ATT_Q_HEADS = 8
ATT_KV_HEADS = 2
ATT_GROUP = ATT_Q_HEADS // ATT_KV_HEADS
ATT_HEAD_DIM = 64
ROPE_DIM = 16
ROPE_THETA = 500000.0
XPOS_THETA = 10000.0
D_FF = 4 * D_MODEL
EPS = 1e-6
NEG_INF = -1e30

RET_QK = RET_HEADS * RET_QK_DIM
RET_V = RET_HEADS * RET_V_DIM
ATT_Q = ATT_Q_HEADS * ATT_HEAD_DIM
ATT_KV = ATT_KV_HEADS * ATT_HEAD_DIM
OFF_QR = 0
OFF_KR = OFF_QR + RET_QK
OFF_VR = OFF_KR + RET_QK
OFF_GR = OFF_VR + RET_V
OFF_QA = OFF_GR + RET_V
OFF_KA = OFF_QA + ATT_Q
OFF_VA = OFF_KA + ATT_KV
MIX_COLS = OFF_VA + ATT_KV
GATE_COLS = 2 * D_MODEL
D_IN = MIX_COLS + GATE_COLS

LANES = 128
MXU_N = 256
VMEM_LIMIT = 60 * 1024 * 1024


def _dot(a, b):
    return jnp.dot(a, b, preferred_element_type=F32)


def _dot_nt(a, b):
    return lax.dot_general(a, b, (((1,), (1,)), ((), ())), preferred_element_type=F32)


def _rms(x, g):
    return x * lax.rsqrt(jnp.mean(x * x, axis=-1, keepdims=True) + EPS) * g


def _sigmoid(x):
    return 1.0 / (1.0 + jnp.exp(-x))


def _resident(shape):
    return pl.BlockSpec(shape, lambda *_: (0,) * len(shape), pipeline_mode=pl.Buffered(1))


def _cast_riders(weights, layer, steps, step_index):
    in_specs, out_specs, out_shapes = [], [], []
    for w in weights:
        _, k, n = w.shape
        rows = k // steps
        assert rows * steps == k and rows % 16 == 0, (k, steps)
        in_specs.append(pl.BlockSpec((None, rows, n), lambda *g: (layer, step_index(*g), 0)))
        out_specs.append(pl.BlockSpec((rows, n), lambda *g: (step_index(*g), 0)))
        out_shapes.append(jax.ShapeDtypeStruct((k, n), BF16))
    return in_specs, out_specs, out_shapes


def _split_refs(refs, n_in, n_cast):
    a, b = n_in + n_cast, n_in + n_cast + 1
    return refs[:n_in], refs[n_in:a], refs[a], refs[b:b + n_cast], refs[b + n_cast:]


def _run_cast_riders(cast_in, cast_out):
    for src, dst in zip(cast_in, cast_out):
        dst[...] = src[...].astype(BF16)


def _layer_resident(shape, layer):
    return pl.BlockSpec((None,) + shape, lambda *_: (layer,) + (0,) * len(shape), pipeline_mode=pl.Buffered(1))


def _rotary_tables(seq):
    pos = jnp.concatenate([jnp.arange(N_META, N_META + seq), jnp.arange(N_META)]).astype(F32)
    d = jnp.arange(LANES) % RET_QK_DIM

    def table(theta, rot_dim):
        half = rot_dim // 2
        freqs = jnp.power(jnp.float32(theta), -jnp.arange(0, rot_dim, 2, dtype=F32) / rot_dim)
        ang = pos[:, None] * freqs[None, :]
        cos, sin = jnp.cos(ang), jnp.sin(ang)
        idx = d % half
        rotated = (d < rot_dim)[None, :]
        cos_l = jnp.where(rotated, cos[:, idx], 1.0)
        sin_l = jnp.where(rotated, jnp.where((d < half)[None, :], -sin[:, idx], sin[:, idx]), 0.0)
        return cos_l, sin_l

    cr, sr = table(XPOS_THETA, RET_QK_DIM)
    ca, sa = table(ROPE_THETA, ROPE_DIM)
    qs_r = RET_QK_DIM ** -0.5
    qs_a = ATT_HEAD_DIM ** -0.5
    return (cr * qs_r, sr * qs_r, cr, sr, ca * qs_a, sa * qs_a, ca, sa)


def _rotate(x, cos, sin, half):
    lane = lax.broadcasted_iota(jnp.int32, x.shape, 1) % RET_QK_DIM
    partner = jnp.where(lane < half, pltpu.roll(x, LANES - half, 1), pltpu.roll(x, half, 1))
    return x * cos + partner * sin


IN_TM = 688
IN_SUB = ((0, 272), (272, 416))


def _in_proj_kernel(*refs, assemble):
    h_ref, g_ref, w_ref, crq, srq, crk, srk, caq, saq, cak, sak = refs[:11]
    if assemble:
        meta_ref, mix_ref, gate_ref, h0_ref = refs[11:]
    else:
        mix_ref, gate_ref = refs[11:]

    def chain(start, rows):
        r = pl.ds(start, rows)
        x = h_ref[r, :]
        if assemble:
            if start + rows == IN_TM:
                last_tile = pl.program_id(1) == pl.num_programs(1) - 1
                row = lax.broadcasted_iota(jnp.int32, (rows, D_MODEL), 0)
                meta_rows = jnp.concatenate([jnp.zeros((rows - N_META, D_MODEL), F32), meta_ref[...]], axis=0)
                x = jnp.where((row >= rows - N_META) & last_tile, meta_rows, x)
            h0_ref[r, :] = x
        u = _rms(x, g_ref[...]).astype(BF16)
        yield

        def proj(c0, n):
            return _dot(u, w_ref[:, c0:c0 + n])

        def rot_store(c0, cos_ref, sin_ref, half):
            p = proj(c0, MXU_N)
            for j in range(MXU_N // LANES):
                t = _rotate(p[:, j * LANES:(j + 1) * LANES], cos_ref[r, :], sin_ref[r, :], half)
                mix_ref[r, c0 + j * LANES:c0 + (j + 1) * LANES] = t.astype(BF16)

        for c0 in range(OFF_QR, OFF_KR, MXU_N):
            rot_store(c0, crq, srq, RET_QK_DIM // 2)
            yield
        for c0 in range(OFF_KR, OFF_VR, MXU_N):
            rot_store(c0, crk, srk, RET_QK_DIM // 2)
            yield
        for c0 in range(OFF_GR, OFF_QA, MXU_N):
            p = proj(c0, MXU_N)
            mix_ref[r, c0:c0 + MXU_N] = (p * _sigmoid(p)).astype(BF16)
            yield
        for c0 in range(OFF_QA, OFF_KA, MXU_N):
            rot_store(c0, caq, saq, ROPE_DIM // 2)
            yield
        p = proj(OFF_KA, MXU_N)
        mix_ref[r, OFF_KA:OFF_VA] = _rotate(p[:, :LANES], cak[r, :], sak[r, :], ROPE_DIM // 2).astype(BF16)
        mix_ref[r, OFF_VA:MIX_COLS] = p[:, LANES:].astype(BF16)
        yield
        for c0 in range(0, GATE_COLS, MXU_N):
            gate_ref[r, c0:c0 + MXU_N] = _sigmoid(proj(MIX_COLS + c0, MXU_N)).astype(BF16)
            yield
        for c0 in range(OFF_VR, OFF_GR, MXU_N):
            mix_ref[r, c0:c0 + MXU_N] = proj(c0, MXU_N).astype(BF16)
            yield

    chains = [chain(start, rows) for start, rows in IN_SUB]
    while chains:
        chains = [c for c in chains if next(c, "done") != "done"]


def _in_proj(h, gains, w, tables, layer, batch, ltok, meta=None):
    assemble = meta is not None
    tiles = ltok // IN_TM
    t = batch * ltok
    tok = lambda cols: pl.BlockSpec((IN_TM, cols), lambda b, i: (b * tiles + i, 0))
    tab = pl.BlockSpec((IN_TM, LANES), lambda b, i: (i, 0))
    h_spec = pl.BlockSpec((None, IN_TM, D_MODEL), lambda b, i: (b, i, 0)) if assemble else tok(D_MODEL)
    in_specs = [h_spec, _layer_resident((1, D_MODEL), layer), _resident((D_MODEL, D_IN))] + [tab] * 8
    out_specs = [tok(MIX_COLS), tok(GATE_COLS)]
    out_shape = [jax.ShapeDtypeStruct((t, MIX_COLS), BF16), jax.ShapeDtypeStruct((t, GATE_COLS), BF16)]
    operands = [h, gains, w, *tables]
    if assemble:
        in_specs.append(_resident((N_META, D_MODEL)))
        out_specs.append(tok(D_MODEL))
        out_shape.append(jax.ShapeDtypeStruct((t, D_MODEL), F32))
        operands.append(meta)
    return pl.pallas_call(
        functools.partial(_in_proj_kernel, assemble=assemble),
        grid=(batch, tiles),
        in_specs=in_specs,
        out_specs=out_specs,
        out_shape=out_shape,
        compiler_params=pltpu.CompilerParams(dimension_semantics=("arbitrary", "arbitrary"),
                                             vmem_limit_bytes=VMEM_LIMIT),
        name="in_proj",
    )(*operands)


RET_PAIRS_PER_STEP = 2


def _ret_kernel(*refs, n_chunks, n_cast):
    (rd_ref, q_ref, k_ref, v_ref, sg_ref), cast_in, o_ref, cast_out, scratch = _split_refs(refs, 5, n_cast)
    _run_cast_riders(cast_in, cast_out)
    for p in range(RET_PAIRS_PER_STEP):
        narrow, wide = pl.ds(p * LANES, LANES), pl.ds(2 * p * LANES, 2 * LANES)
        _ret_pair(rd_ref.at[p], q_ref.at[:, narrow], k_ref.at[:, narrow], v_ref.at[:, wide], sg_ref.at[:, wide],
                  o_ref.at[:, wide], [s.at[p] for s in scratch], n_chunks)


def _ret_pair(rd_ref, q_ref, k_ref, v_ref, sg_ref, o_ref, scratch, n_chunks):
    tab_scr, dec_scr, pad_scr, padw_scr, kv_scr, st_scr, s_scr = scratch
    C = CHUNK
    seq = n_chunks * C
    W = 2 * LANES
    H = RET_QK_DIM
    lg = -jnp.exp(rd_ref[...])
    lgf_a, lgf_b, lgb_a, lgb_b, lgf_p, lgb_p = [lg[i:i + 1, :] for i in range(6)]

    row = lax.broadcasted_iota(jnp.int32, (C, LANES), 0)
    col = lax.broadcasted_iota(jnp.int32, (C, LANES), 1)
    a = row.astype(F32)
    rel = (row - col).astype(F32)

    def decay_matrix(lf, lb):
        return jnp.where(rel >= 0, jnp.exp(lf * jnp.maximum(rel, 0.0)), jnp.exp(lb * jnp.maximum(-rel, 0.0)))

    tab_scr[0] = jnp.concatenate([jnp.exp(lgf_p * (a + 1.0)), jnp.exp(lgb_p * (C - a))], axis=1)
    tab_scr[1] = jnp.concatenate([jnp.exp(lgf_p * (C - 1.0 - a)), jnp.exp(lgb_p * a)], axis=1)
    tab_scr[2] = jnp.concatenate([decay_matrix(lgf_a, lgb_a), decay_matrix(lgf_b, lgb_b)], axis=1)
    top = row < H
    for i, (la, lb) in enumerate(((lgf_a, lgf_b), (lgb_a, lgb_b))):
        g = jnp.where(top, jnp.exp(la * C), jnp.exp(lb * C))
        dec_scr[i] = jnp.concatenate([g, g], axis=1)
    lo = col < H

    pad_rows = C - N_META
    for i, ref in enumerate((q_ref, k_ref)):
        pad_scr[i, 0:pad_rows, :] = jnp.zeros((pad_rows, LANES), BF16)
        pad_scr[i, pad_rows:C, :] = ref[seq:seq + N_META, :]
    for i, ref in enumerate((v_ref, sg_ref)):
        padw_scr[i, 0:pad_rows, :] = jnp.zeros((pad_rows, W), BF16)
        padw_scr[i, pad_rows:C, :] = ref[seq:seq + N_META, :]

    def seq_rows(c):
        return pl.ds(c * C, C)

    def kv(kc, vc):
        kf = kc.astype(F32)
        ks = jnp.concatenate([kf, kf], axis=1) * tab_scr[1]
        kt = jnp.concatenate([ks[:, :LANES].T, ks[:, LANES:].T], axis=0).astype(BF16)
        x = _dot(kt, vc)
        z = jnp.zeros((H, LANES), F32)
        rows = []
        for d in range(2):
            rows.append(jnp.concatenate([x[d * C:d * C + H, :LANES], z], axis=1))
            rows.append(jnp.concatenate([z, x[d * C + H:(d + 1) * C, LANES:]], axis=1))
        return jnp.concatenate(rows, axis=0)

    kv_scr[0] = kv(pad_scr[1], padw_scr[0])
    for c in range(n_chunks):
        r = seq_rows(c)
        kv_scr[c + 1] = kv(k_ref[r, :], v_ref[r, :])

    s_scr[...] = jnp.zeros((2, C, W), F32)
    for t in range(n_chunks + 1):
        u = n_chunks - t
        sf = s_scr[0]
        sb = s_scr[1]
        st_scr[t, 0:C, :] = sf.astype(BF16)
        st_scr[u, C:2 * C, :] = sb.astype(BF16)
        s_scr[0] = sf * dec_scr[0] + kv_scr[t, 0:C, :]
        s_scr[1] = sb * dec_scr[1] + kv_scr[u, C:2 * C, :]

    def emit(qc, kc, vc, sgc, st):
        qf = qc.astype(F32)
        kf = kc.astype(F32)
        k2 = jnp.concatenate([jnp.where(lo, kf, 0.0), jnp.where(lo, 0.0, kf)], axis=0).astype(BF16)
        p = (_dot_nt(qc, k2) * tab_scr[2]).astype(BF16)
        zb = jnp.zeros((C, LANES), BF16)
        vbd = jnp.concatenate([jnp.concatenate([vc[:, :LANES], zb], axis=1),
                               jnp.concatenate([zb, vc[:, LANES:]], axis=1)], axis=0)
        qs = (jnp.concatenate([qf, qf], axis=1) * tab_scr[0]).astype(BF16)
        o = _dot(p, vbd) + _dot(qs, st)
        halves = []
        for j in range(2):
            oh = o[:, j * LANES:(j + 1) * LANES]
            mu = jnp.mean(oh, axis=-1, keepdims=True)
            var = jnp.mean(jnp.square(oh - mu), axis=-1, keepdims=True)
            halves.append((oh - mu) * lax.rsqrt(var + EPS))
        return (jnp.concatenate(halves, axis=1) * sgc.astype(F32)).astype(BF16)

    for c in range(n_chunks):
        r = seq_rows(c)
        o_ref[r, :] = emit(q_ref[r, :], k_ref[r, :], v_ref[r, :], sg_ref[r, :], st_scr[c + 1])
    om = emit(pad_scr[0], pad_scr[1], padw_scr[0], padw_scr[1], st_scr[0])
    o_ref[seq:seq + N_META, :] = om[pad_rows:, :]


def _retention(mix, rd_lanes, batch, ltok, cast_weights, layer):
    n_chunks = (ltok - N_META) // CHUNK
    pp = RET_PAIRS_PER_STEP
    steps = RET_HEADS // 2 // pp
    narrow, wide = pp * LANES, 2 * pp * LANES
    blk = lambda cols, base: pl.BlockSpec((ltok, cols), lambda b, j: (b, base + j))
    c_in, c_out, c_shapes = _cast_riders(cast_weights, layer, batch * steps, lambda b, j: b * steps + j)
    per_pair = lambda *shape: pltpu.VMEM((pp,) + shape[:-1], shape[-1])
    return pl.pallas_call(
        functools.partial(_ret_kernel, n_chunks=n_chunks, n_cast=len(cast_weights)),
        grid=(batch, steps),
        in_specs=[
            pl.BlockSpec((pp, 8, LANES), lambda b, j: (j, 0, 0)),
            blk(narrow, OFF_QR // narrow),
            blk(narrow, OFF_KR // narrow),
            blk(wide, OFF_VR // wide),
            blk(wide, OFF_GR // wide),
        ] + c_in,
        out_specs=[pl.BlockSpec((ltok, wide), lambda b, j: (b, j))] + c_out,
        out_shape=[jax.ShapeDtypeStruct((batch * ltok, RET_V), BF16)] + c_shapes,
        scratch_shapes=[
            per_pair(3, CHUNK, 2 * LANES, F32),
            per_pair(2, CHUNK, 2 * LANES, F32),
            per_pair(2, CHUNK, LANES, BF16),
            per_pair(2, CHUNK, 2 * LANES, BF16),
            per_pair(n_chunks + 1, 2 * CHUNK, 2 * LANES, F32),
            per_pair(n_chunks + 1, 2 * CHUNK, 2 * LANES, BF16),
            per_pair(2, CHUNK, 2 * LANES, F32),
        ],
        compiler_params=pltpu.CompilerParams(dimension_semantics=("arbitrary", "arbitrary"),
                                             vmem_limit_bytes=VMEM_LIMIT),
        name="retention",
    )(rd_lanes, mix, mix, mix, mix, *cast_weights)


def _decay_lanes(ret_decay_l):
    rd = ret_decay_l.astype(F32)
    pairs = RET_HEADS // 2
    fa, fb = rd[0, 0::2], rd[0, 1::2]
    ba, bb = rd[1, 0::2], rd[1, 1::2]
    full = lambda v: jnp.broadcast_to(v[:, None], (pairs, LANES))
    split = lambda x, y: jnp.concatenate([jnp.broadcast_to(x[:, None], (pairs, RET_QK_DIM)),
                                          jnp.broadcast_to(y[:, None], (pairs, RET_QK_DIM))], axis=1)
    rows = [full(fa), full(fb), full(ba), full(bb), split(fa, fb), split(ba, bb), full(fa), full(fa)]
    return jnp.stack(rows, axis=1)


BAND = 3 * CHUNK
KEYS = BAND + CHUNK


MASK_UPPER, MASK_LOWER, MASK_METAQ, MASK_NONE, MASK_META = range(5)
MASK_ALL = None
N_MASKS = MASK_META + ATT_GROUP
SOFTMAX_ROWS = 32


def _att_kernel(*refs, n_blocks, n_cast):
    (sink_ref, q_ref, k_ref, v_ref), cast_in, o_ref, cast_out, scratch = _split_refs(refs, 4, n_cast)
    _run_cast_riders(cast_in, cast_out)
    k_scr, v_scr, *shared = scratch
    for g in range(ATT_KV_HEADS):
        cols = pl.ds(2 * g * LANES, 2 * LANES)
        _att_group(g, sink_ref, q_ref.at[:, cols], k_ref, v_ref, o_ref.at[:, cols],
                   [k_scr.at[g], v_scr.at[g]] + shared, n_blocks)


def _att_group(g, sink_ref, q_ref, k_ref, v_ref, o_ref, scratch, n_blocks):
    k_scr, v_scr, bias_scr, sa_scr, sb_scr, e_scr = scratch
    C = CHUNK
    seq = n_blocks * C
    ltok = seq + N_META
    lane = lax.broadcasted_iota(jnp.int32, (ltok, LANES), 1)
    lo = lane < ATT_HEAD_DIM

    def own_head_low(ref):
        x = ref[...].astype(F32)
        return x if g == 0 else pltpu.roll(x, ATT_HEAD_DIM, 1)

    kk = own_head_low(k_ref)
    kdup = jnp.where(lo, kk, pltpu.roll(kk, ATT_HEAD_DIM, 1)).astype(BF16)
    vone = jnp.where(lo, own_head_low(v_ref), 1.0).astype(BF16)
    tail_row = lax.broadcasted_iota(jnp.int32, (C - N_META, LANES), 0)
    tail_lane = lax.broadcasted_iota(jnp.int32, (C - N_META, LANES), 1)
    sink_value = jnp.where((tail_row == 0) & (tail_lane >= ATT_HEAD_DIM), 1.0, 0.0).astype(BF16)
    for scr, val, tail in ((k_scr, kdup, jnp.zeros((C - N_META, LANES), BF16)), (v_scr, vone, sink_value)):
        meta_tile = jnp.concatenate([val[seq:ltok, :], tail], axis=0)
        for j in range(n_blocks):
            t = j + j // 2
            scr[t * C:(t + 1) * C, :] = val[j * C:(j + 1) * C, :]
            if j % 2 == 1:
                scr[(t + 1) * C:(t + 2) * C, :] = meta_tile

    row = lax.broadcasted_iota(jnp.int32, (C, LANES), 0)
    col = lax.broadcasted_iota(jnp.int32, (C, LANES), 1)
    visible = lambda cond: jnp.where(cond, 0.0, NEG_INF)
    bias_scr[MASK_UPPER] = visible(col >= row)
    bias_scr[MASK_LOWER] = visible(col <= row)
    bias_scr[MASK_METAQ] = visible(col <= row + (C - N_META))
    bias_scr[MASK_NONE] = visible(col < 0)
    for h in range(ATT_GROUP):
        sink = sink_ref[g * ATT_GROUP + h]
        bias_scr[MASK_META + h] = jnp.where(col < N_META, 0.0, jnp.where(col == N_META, sink, NEG_INF))

    def scores(q4, slab_start, masks, s_scr, slot):
        R = q4.shape[0]
        qf = q4.astype(F32)
        lo_r = lax.broadcasted_iota(jnp.int32, (R, LANES), 1) < ATT_HEAD_DIM
        stacked = []
        for p in range(ATT_GROUP // 2):
            qp = qf[:, p * LANES:(p + 1) * LANES]
            stacked += [jnp.where(lo_r, qp, 0.0), jnp.where(lo_r, 0.0, qp)]
        qs = jnp.concatenate(stacked, axis=0).astype(BF16)

        def bias_tile(h, mask):
            if mask is MASK_ALL:
                return jnp.zeros((R, LANES), F32)
            return bias_scr[MASK_META + h if mask == MASK_META else mask, 0:R, :]

        bias = jnp.concatenate([jnp.concatenate([bias_tile(h, mask) for mask in masks], axis=1)
                                for h in range(ATT_GROUP)], axis=0)
        s_scr[slot, 0:ATT_GROUP * R, :] = _dot_nt(qs, k_scr[pl.ds(slab_start, KEYS), :]) + bias

    def finish(R, slab_start, s_scr, slot):
        eslot = slot if s_scr is sa_scr else 2 + slot
        SR = min(R, SOFTMAX_ROWS)
        lo_r = lax.broadcasted_iota(jnp.int32, (R, LANES), 1) < ATT_HEAD_DIM
        slab = pl.ds(slab_start, KEYS)
        for r0 in range(0, ATT_GROUP * R, SR):
            rows = pl.ds(r0, SR)
            tiles = [s_scr[slot, rows, t * LANES:(t + 1) * LANES] for t in range(KEYS // LANES)]
            top = jnp.maximum(jnp.maximum(tiles[0], tiles[1]), jnp.maximum(tiles[2], tiles[3]))
            m = jnp.max(top, axis=-1, keepdims=True)
            e_scr[eslot, rows, :] = jnp.exp((s_scr[slot, rows, :] - m).astype(BF16))
        res = _dot(e_scr[eslot, 0:ATT_GROUP * R, :], v_scr[slab, :])
        outs = []
        for p in range(ATT_GROUP // 2):
            r_ev, r_od = res[2 * p * R:(2 * p + 1) * R, :], res[(2 * p + 1) * R:(2 * p + 2) * R, :]
            num = jnp.where(lo_r, r_ev, pltpu.roll(r_od, ATT_HEAD_DIM, 1))
            den = jnp.where(lo_r, pltpu.roll(r_ev, ATT_HEAD_DIM, 1), r_od)
            outs.append(num / den)
        return jnp.concatenate(outs, axis=1).astype(BF16)

    odd_masks = (MASK_UPPER, MASK_ALL, MASK_META, MASK_LOWER)
    even_masks = (MASK_UPPER, MASK_META, MASK_ALL, MASK_LOWER)
    n_pairs = n_blocks // 2 - 1

    def q_rows(n):
        return pl.ds(pl.multiple_of(n * C, C), C)

    def pair_scores(i, s_scr):
        scores(q_ref[q_rows(2 * i + 1), :], pl.multiple_of(3 * i * C, C), odd_masks, s_scr, 0)
        scores(q_ref[q_rows(2 * i + 2), :], pl.multiple_of((3 * i + 1) * C, C), even_masks, s_scr, 1)

    def pair_finish(i, s_scr):
        o_ref[q_rows(2 * i + 1), :] = finish(C, pl.multiple_of(3 * i * C, C), s_scr, 0)
        o_ref[q_rows(2 * i + 2), :] = finish(C, pl.multiple_of((3 * i + 1) * C, C), s_scr, 1)

    pair_scores(0, sa_scr)
    scores(q_ref[0:C, :], 0, (MASK_ALL, MASK_LOWER, MASK_META, MASK_NONE), sb_scr, 0)
    o_ref[0:C, :] = finish(C, 0, sb_scr, 0)

    def body(j, carry):
        pair_scores(2 * j + 1, sb_scr)
        pair_finish(2 * j, sa_scr)
        pair_scores(2 * j + 2, sa_scr)
        pair_finish(2 * j + 1, sb_scr)
        return carry

    lax.fori_loop(0, (n_pairs - 1) // 2, body, 0)
    last = n_blocks - 1
    last_slab = ((last - 2) + (last - 2) // 2) * C
    scores(q_ref[last * C:seq, :], last_slab, (MASK_NONE, MASK_META, MASK_UPPER, MASK_ALL), sb_scr, 0)
    scores(q_ref[seq:ltok, :], 0, (MASK_METAQ, MASK_NONE, MASK_META, MASK_NONE), sb_scr, 1)
    pair_finish(n_pairs - 1, sa_scr)
    o_ref[last * C:seq, :] = finish(C, last_slab, sb_scr, 0)
    o_ref[seq:ltok, :] = finish(N_META, 0, sb_scr, 1)


def _attention(mix, sink, batch, ltok, cast_weights, layer):
    n_blocks = (ltok - N_META) // CHUNK
    assert n_blocks % 4 == 0
    slab_rows = (n_blocks + n_blocks // 2) * CHUNK
    stacked = ATT_GROUP * CHUNK
    c_in, c_out, c_shapes = _cast_riders(cast_weights, layer, batch, lambda b: b)
    return pl.pallas_call(
        functools.partial(_att_kernel, n_blocks=n_blocks, n_cast=len(cast_weights)),
        grid=(batch,),
        in_specs=[
            pl.BlockSpec(memory_space=pltpu.SMEM),
            pl.BlockSpec((ltok, ATT_Q), lambda b: (b, OFF_QA // ATT_Q)),
            pl.BlockSpec((ltok, LANES), lambda b: (b, OFF_KA // LANES)),
            pl.BlockSpec((ltok, LANES), lambda b: (b, OFF_VA // LANES)),
        ] + c_in,
        out_specs=[pl.BlockSpec((ltok, ATT_Q), lambda b: (b, 0))] + c_out,
        out_shape=[jax.ShapeDtypeStruct((batch * ltok, ATT_Q), BF16)] + c_shapes,
        scratch_shapes=[
            pltpu.VMEM((ATT_KV_HEADS, slab_rows, LANES), BF16),
            pltpu.VMEM((ATT_KV_HEADS, slab_rows, LANES), BF16),
            pltpu.VMEM((N_MASKS, CHUNK, LANES), F32),
            pltpu.VMEM((2, stacked, KEYS), F32),
            pltpu.VMEM((2, stacked, KEYS), F32),
            pltpu.VMEM((4, stacked, KEYS), BF16),
        ],
        compiler_params=pltpu.CompilerParams(dimension_semantics=("arbitrary",), vmem_limit_bytes=VMEM_LIMIT),
        name="attention",
    )(sink, mix, mix, mix, *cast_weights)


POST_TM = 688
POST_SUB = ((0, 416), (416, 272))
FF_CHUNK = 1024


def _post_ffn_kernel(h_ref, yr_ref, ya_ref, gt_ref, wr_ref, wa_ref, wm_ref, w1_ref, w2_ref,
                     gpost_ref, gpre_ref, gffpost_ref, o_ref):
    def chain(r):
        y_r = _dot(yr_ref[r, :], wr_ref[...])
        y_a = _dot(ya_ref[r, :], wa_ref[...])
        z = gt_ref[r, :D_MODEL].astype(F32) * y_r + gt_ref[r, D_MODEL:].astype(F32) * y_a
        yield
        mix = _dot(z.astype(BF16), wm_ref[...])
        yield
        h1 = h_ref[r, :] + _rms(mix, gpost_ref[...])
        u = _rms(h1, gpre_ref[...]).astype(BF16)
        yield
        ff = jnp.zeros(h1.shape, F32)
        for c0 in range(0, D_FF, FF_CHUNK):
            hid = jnp.maximum(_dot(u, w1_ref[:, c0:c0 + FF_CHUNK]), 0.0)
            ff = ff + _dot((hid * hid).astype(BF16), w2_ref[c0:c0 + FF_CHUNK, :])
            yield
        o_ref[r, :] = h1 + _rms(ff, gffpost_ref[...])

    chains = [chain(pl.ds(start, rows)) for start, rows in POST_SUB]
    while chains:
        chains = [c for c in chains if next(c, "done") != "done"]


def _post_ffn(h, y_r, y_a, gates, wr, wa, wm, w1, w2, gpost, gpre, gffpost, layer, batch, ltok, out_rows):
    tiles = ltok // POST_TM
    tok = lambda cols: pl.BlockSpec((POST_TM, cols), lambda b, i: (b * tiles + i, 0))
    gain = _layer_resident((1, D_MODEL), layer)
    return pl.pallas_call(
        _post_ffn_kernel,
        grid=(batch, tiles),
        in_specs=[tok(D_MODEL), tok(RET_V), tok(ATT_Q), tok(GATE_COLS),
                  _resident((RET_V, D_MODEL)), _resident((ATT_Q, D_MODEL)), _resident((D_MODEL, D_MODEL)),
                  _resident((D_MODEL, D_FF)), _resident((D_FF, D_MODEL)), gain, gain, gain],
        out_specs=pl.BlockSpec((None, POST_TM, D_MODEL), lambda b, i: (b, i, 0)),
        out_shape=jax.ShapeDtypeStruct((batch, out_rows, D_MODEL), F32),
        compiler_params=pltpu.CompilerParams(dimension_semantics=("arbitrary", "arbitrary"),
                                             vmem_limit_bytes=VMEM_LIMIT),
        name="post_ffn",
    )(h, y_r, y_a, gates, wr, wa, wm, w1, w2, gpost, gpre, gffpost)


def kernel(x, meta_tokens, w_in, w_ret_o, w_att_o, w_mix_o, w_ff1, w_ff2,
           norm_mix_pre, norm_mix_post, norm_ff_pre, norm_ff_post, ret_decay, attn_sink):
    batch, seq, d = x.shape
    depth = w_in.shape[0]
    ltok = seq + N_META
    assert d == D_MODEL and seq % (2 * CHUNK) == 0 and seq >= BAND and ltok % IN_TM == 0 and ltok % POST_TM == 0
    tables = _rotary_tables(seq)
    gains = [g.astype(F32).reshape(depth, 1, d) for g in (norm_mix_pre, norm_mix_post, norm_ff_pre, norm_ff_post)]
    post_weights = tuple(w.astype(F32) for w in (w_ret_o, w_att_o, w_mix_o, w_ff1, w_ff2))
    w_in_f = w_in.astype(F32)
    w_in_b = w_in_f[0].astype(BF16)
    for l in range(depth):
        last = l == depth - 1
        if l == 0:
            mix, gates, h = _in_proj(x.astype(F32), gains[0], w_in_b, tables, l, batch, ltok,
                                     meta=meta_tokens.astype(F32))
        else:
            mix, gates = _in_proj(h, gains[0], w_in_b, tables, l, batch, ltok)
        y_r, *post_b = _retention(mix, _decay_lanes(ret_decay[l]), batch, ltok, post_weights, l)
        y_a, *next_in = _attention(mix, attn_sink[l].astype(F32), batch, ltok, () if last else (w_in_f,), l + 1)
        if not last:
            w_in_b, = next_in
        h = _post_ffn(h, y_r, y_a, gates, *post_b,
                      gains[1], gains[2], gains[3], l, batch, ltok, seq if last else ltok)
        if not last:
            h = h.reshape(batch * ltok, d)
    return h
```

```python
import functools

import jax
import jax.numpy as jnp
from jax import lax
from jax.experimental import pallas as pl
from jax.experimental.pallas import tpu as pltpu

F32 = jnp.float32
BF16 = jnp.bfloat16

D_MODEL = 1024
N_META = 16
CHUNK = 128
RET_HEADS = 8
RET_QK_DIM = 64
RET_V_DIM = 128
ATT_Q_HEADS = 8
ATT_KV_HEADS = 2
ATT_GROUP = ATT_Q_HEADS // ATT_KV_HEADS
ATT_HEAD_DIM = 64
ROPE_DIM = 16
ROPE_THETA = 500000.0
XPOS_THETA = 10000.0
D_FF = 4 * D_MODEL
EPS = 1e-6
NEG_INF = -1e30

RET_QK = RET_HEADS * RET_QK_DIM
RET_V = RET_HEADS * RET_V_DIM
ATT_Q = ATT_Q_HEADS * ATT_HEAD_DIM
ATT_KV = ATT_KV_HEADS * ATT_HEAD_DIM
OFF_QR = 0
OFF_KR = OFF_QR + RET_QK
OFF_VR = OFF_KR + RET_QK
OFF_GR = OFF_VR + RET_V
OFF_QA = OFF_GR + RET_V
OFF_KA = OFF_QA + ATT_Q
OFF_VA = OFF_KA + ATT_KV
MIX_COLS = OFF_VA + ATT_KV
GATE_COLS = 2 * D_MODEL
D_IN = MIX_COLS + GATE_COLS
OUT_KA = OFF_KA
OUT_VA = OUT_KA + ATT_KV_HEADS * 2 * ATT_HEAD_DIM
OUT_COLS = OUT_VA + ATT_KV_HEADS * 2 * ATT_HEAD_DIM

LANES = 128
MXU_N = 256
VMEM_LIMIT = 60 * 1024 * 1024


def _dot(a, b):
    return jnp.dot(a, b, preferred_element_type=F32)


def _dot_nt(a, b):
    return lax.dot_general(a, b, (((1,), (1,)), ((), ())), preferred_element_type=F32)


def _rms(x, g):
    return x * lax.rsqrt(jnp.mean(x * x, axis=-1, keepdims=True) + EPS) * g


def _sigmoid(x):
    return 1.0 / (1.0 + jnp.exp(-x))


def _resident(shape):
    return pl.BlockSpec(shape, lambda *_: (0,) * len(shape), pipeline_mode=pl.Buffered(1))


def _cast_riders(weights, layer, steps, step_index):
    in_specs, out_specs, out_shapes = [], [], []
    for w in weights:
        _, k, n = w.shape
        rows = k // steps
        assert rows * steps == k and rows % 16 == 0, (k, steps)
        in_specs.append(pl.BlockSpec((None, rows, n), lambda *g: (layer, step_index(*g), 0)))
        out_specs.append(pl.BlockSpec((rows, n), lambda *g: (step_index(*g), 0)))
        out_shapes.append(jax.ShapeDtypeStruct((k, n), BF16))
    return in_specs, out_specs, out_shapes


def _split_refs(refs, n_in, n_cast):
    a, b = n_in + n_cast, n_in + n_cast + 1
    return refs[:n_in], refs[n_in:a], refs[a], refs[b:b + n_cast], refs[b + n_cast:]


def _run_cast_riders(cast_in, cast_out):
    for src, dst in zip(cast_in, cast_out):
        dst[...] = src[...].astype(BF16)


def _layer_resident(shape, layer):
    return pl.BlockSpec((None,) + shape, lambda *_: (layer,) + (0,) * len(shape), pipeline_mode=pl.Buffered(1))


def _rotary_tables(seq):
    pos = jnp.concatenate([jnp.arange(N_META, N_META + seq), jnp.arange(N_META)]).astype(F32)
    d = jnp.arange(LANES) % RET_QK_DIM

    def table(theta, rot_dim):
        half = rot_dim // 2
        freqs = jnp.power(jnp.float32(theta), -jnp.arange(0, rot_dim, 2, dtype=F32) / rot_dim)
        ang = pos[:, None] * freqs[None, :]
        cos, sin = jnp.cos(ang), jnp.sin(ang)
        idx = d % half
        rotated = (d < rot_dim)[None, :]
        cos_l = jnp.where(rotated, cos[:, idx], 1.0)
        sin_l = jnp.where(rotated, jnp.where((d < half)[None, :], -sin[:, idx], sin[:, idx]), 0.0)
        return cos_l, sin_l

    cr, sr = table(XPOS_THETA, RET_QK_DIM)
    ca, sa = table(ROPE_THETA, ROPE_DIM)
    qs_r = RET_QK_DIM ** -0.5
    qs_a = ATT_HEAD_DIM ** -0.5
    return (cr * qs_r, sr * qs_r, cr, sr, ca * qs_a, sa * qs_a, ca, sa)


def _rotate(x, cos, sin, half):
    lane = lax.broadcasted_iota(jnp.int32, x.shape, 1) % RET_QK_DIM
    partner = jnp.where(lane < half, pltpu.roll(x, LANES - half, 1), pltpu.roll(x, half, 1))
    return x * cos + partner * sin


IN_TM = 688
IN_SUB = ((0, 272), (272, 416))


def _in_proj_kernel(*refs, assemble):
    h_ref, g_ref, w_ref, crq, srq, crk, srk, caq, saq, cak, sak = refs[:11]
    if assemble:
        meta_ref, mix_ref, gate_ref, h0_ref = refs[11:]
    else:
        mix_ref, gate_ref = refs[11:]

    def chain(start, rows):
        r = pl.ds(start, rows)
        x = h_ref[r, :]
        if assemble:
            if start + rows == IN_TM:
                last_tile = pl.program_id(1) == pl.num_programs(1) - 1
                row = lax.broadcasted_iota(jnp.int32, (rows, D_MODEL), 0)
                meta_rows = jnp.concatenate([jnp.zeros((rows - N_META, D_MODEL), F32), meta_ref[...]], axis=0)
                x = jnp.where((row >= rows - N_META) & last_tile, meta_rows, x)
            h0_ref[r, :] = x
        u = (x * g_ref[...]).astype(BF16)
        inv = jnp.broadcast_to(lax.rsqrt(jnp.mean(x * x, axis=-1, keepdims=True) + EPS), (rows, LANES))
        yield

        def proj(c0, n):
            return _dot(u, w_ref[:, c0:c0 + n]) * jnp.concatenate([inv] * (n // LANES), axis=1)

        def rot_store(c0, cos_ref, sin_ref, half):
            p = proj(c0, MXU_N)
            for j in range(MXU_N // LANES):
                t = _rotate(p[:, j * LANES:(j + 1) * LANES], cos_ref[r, :], sin_ref[r, :], half)
                mix_ref[r, c0 + j * LANES:c0 + (j + 1) * LANES] = t.astype(BF16)

        for c0 in range(OFF_QR, OFF_KR, MXU_N):
            rot_store(c0, crq, srq, RET_QK_DIM // 2)
            yield
        for c0 in range(OFF_KR, OFF_VR, MXU_N):
            rot_store(c0, crk, srk, RET_QK_DIM // 2)
            yield
        for c0 in range(OFF_GR, OFF_QA, MXU_N):
            p = proj(c0, MXU_N)
            mix_ref[r, c0:c0 + MXU_N] = (p * _sigmoid(p)).astype(BF16)
            yield
        for c0 in range(OFF_QA, OFF_KA, MXU_N):
            rot_store(c0, caq, saq, ROPE_DIM // 2)
            yield
        p = proj(OFF_KA, MXU_N)
        k_heads = _rotate(p[:, :LANES], cak[r, :], sak[r, :], ROPE_DIM // 2)
        v_heads = p[:, LANES:]
        low = lax.broadcasted_iota(jnp.int32, (rows, LANES), 1) < ATT_HEAD_DIM
        k_swapped = pltpu.roll(k_heads, ATT_HEAD_DIM, 1)
        v_swapped = pltpu.roll(v_heads, ATT_HEAD_DIM, 1)
        for g, (kg, vg) in enumerate(((jnp.where(low, k_heads, k_swapped), jnp.where(low, v_heads, 1.0)),
                                      (jnp.where(low, k_swapped, k_heads), jnp.where(low, v_swapped, 1.0)))):
            mix_ref[r, OUT_KA + g * LANES:OUT_KA + (g + 1) * LANES] = kg.astype(BF16)
            mix_ref[r, OUT_VA + g * LANES:OUT_VA + (g + 1) * LANES] = vg.astype(BF16)
        yield
        for c0 in range(0, GATE_COLS, MXU_N):
            gate_ref[r, c0:c0 + MXU_N] = _sigmoid(proj(MIX_COLS + c0, MXU_N)).astype(BF16)
            yield
        for c0 in range(OFF_VR, OFF_GR, MXU_N):
            mix_ref[r, c0:c0 + MXU_N] = proj(c0, MXU_N).astype(BF16)
            yield

    chains = [chain(start, rows) for start, rows in IN_SUB]
    while chains:
        chains = [c for c in chains if next(c, "done") != "done"]


def _in_proj(h, gains, w, tables, layer, batch, ltok, meta=None):
    assemble = meta is not None
    tiles = ltok // IN_TM
    t = batch * ltok
    tok = lambda cols: pl.BlockSpec((IN_TM, cols), lambda b, i: (b * tiles + i, 0))
    tab = pl.BlockSpec((IN_TM, LANES), lambda b, i: (i, 0))
    h_spec = pl.BlockSpec((None, IN_TM, D_MODEL), lambda b, i: (b, i, 0)) if assemble else tok(D_MODEL)
    in_specs = [h_spec, _layer_resident((1, D_MODEL), layer), _resident((D_MODEL, D_IN))] + [tab] * 8
    out_specs = [tok(OUT_COLS), tok(GATE_COLS)]
    out_shape = [jax.ShapeDtypeStruct((t, OUT_COLS), BF16), jax.ShapeDtypeStruct((t, GATE_COLS), BF16)]
    operands = [h, gains, w, *tables]
    if assemble:
        in_specs.append(_resident((N_META, D_MODEL)))
        out_specs.append(tok(D_MODEL))
        out_shape.append(jax.ShapeDtypeStruct((t, D_MODEL), F32))
        operands.append(meta)
    return pl.pallas_call(
        functools.partial(_in_proj_kernel, assemble=assemble),
        grid=(batch, tiles),
        in_specs=in_specs,
        out_specs=out_specs,
        out_shape=out_shape,
        compiler_params=pltpu.CompilerParams(dimension_semantics=("arbitrary", "arbitrary"),
                                             vmem_limit_bytes=VMEM_LIMIT),
        name="in_proj",
    )(*operands)


RET_PAIRS_PER_STEP = 2


def _ret_kernel(*refs, n_chunks, n_cast):
    (rd_ref, q_ref, k_ref, v_ref, sg_ref), cast_in, o_ref, cast_out, scratch = _split_refs(refs, 5, n_cast)
    _run_cast_riders(cast_in, cast_out)
    for p in range(RET_PAIRS_PER_STEP):
        narrow, wide = pl.ds(p * LANES, LANES), pl.ds(2 * p * LANES, 2 * LANES)
        _ret_pair(rd_ref.at[p], q_ref.at[:, narrow], k_ref.at[:, narrow], v_ref.at[:, wide], sg_ref.at[:, wide],
                  o_ref.at[:, wide], [s.at[p] for s in scratch], n_chunks)


def _ret_pair(rd_ref, q_ref, k_ref, v_ref, sg_ref, o_ref, scratch, n_chunks):
    tab_scr, dec_scr, pad_scr, padw_scr, kv_scr, st_scr, s_scr = scratch
    C = CHUNK
    seq = n_chunks * C
    W = 2 * LANES
    H = RET_QK_DIM
    lg = -jnp.exp(rd_ref[...])
    lgf_a, lgf_b, lgb_a, lgb_b, lgf_p, lgb_p = [lg[i:i + 1, :] for i in range(6)]

    row = lax.broadcasted_iota(jnp.int32, (C, LANES), 0)
    col = lax.broadcasted_iota(jnp.int32, (C, LANES), 1)
    a = row.astype(F32)
    rel = (row - col).astype(F32)

    def decay_matrix(lf, lb):
        return jnp.where(rel >= 0, jnp.exp(lf * jnp.maximum(rel, 0.0)), jnp.exp(lb * jnp.maximum(-rel, 0.0)))

    tab_scr[0] = jnp.concatenate([jnp.exp(lgf_p * (a + 1.0)), jnp.exp(lgb_p * (C - a))], axis=1)
    tab_scr[1] = jnp.concatenate([jnp.exp(lgf_p * (C - 1.0 - a)), jnp.exp(lgb_p * a)], axis=1)
    tab_scr[2] = jnp.concatenate([decay_matrix(lgf_a, lgb_a), decay_matrix(lgf_b, lgb_b)], axis=1)
    top = row < H
    for i, (la, lb) in enumerate(((lgf_a, lgf_b), (lgb_a, lgb_b))):
        g = jnp.where(top, jnp.exp(la * C), jnp.exp(lb * C))
        dec_scr[i] = jnp.concatenate([g, g], axis=1)
    lo = col < H

    pad_rows = C - N_META
    for i, ref in enumerate((q_ref, k_ref)):
        pad_scr[i, 0:pad_rows, :] = jnp.zeros((pad_rows, LANES), BF16)
        pad_scr[i, pad_rows:C, :] = ref[seq:seq + N_META, :]
    for i, ref in enumerate((v_ref, sg_ref)):
        padw_scr[i, 0:pad_rows, :] = jnp.zeros((pad_rows, W), BF16)
        padw_scr[i, pad_rows:C, :] = ref[seq:seq + N_META, :]

    def seq_rows(c):
        return pl.ds(c * C, C)

    def kv(kc, vc):
        kf = kc.astype(F32)
        ks = jnp.concatenate([kf, kf], axis=1) * tab_scr[1]
        kt = jnp.concatenate([ks[:, :LANES].T, ks[:, LANES:].T], axis=0).astype(BF16)
        x = _dot(kt, vc)
        z = jnp.zeros((H, LANES), F32)
        rows = []
        for d in range(2):
            rows.append(jnp.concatenate([x[d * C:d * C + H, :LANES], z], axis=1))
            rows.append(jnp.concatenate([z, x[d * C + H:(d + 1) * C, LANES:]], axis=1))
        return jnp.concatenate(rows, axis=0)

    kv_scr[0] = kv(pad_scr[1], padw_scr[0])
    for c in range(n_chunks):
        r = seq_rows(c)
        kv_scr[c + 1] = kv(k_ref[r, :], v_ref[r, :])

    s_scr[...] = jnp.zeros((2, C, W), F32)
    for t in range(n_chunks + 1):
        u = n_chunks - t
        sf = s_scr[0]
        sb = s_scr[1]
        st_scr[t, 0:C, :] = sf.astype(BF16)
        st_scr[u, C:2 * C, :] = sb.astype(BF16)
        s_scr[0] = sf * dec_scr[0] + kv_scr[t, 0:C, :]
        s_scr[1] = sb * dec_scr[1] + kv_scr[u, C:2 * C, :]

    def emit(qc, kc, vc, sgc, st):
        qf = qc.astype(F32)
        kf = kc.astype(F32)
        k2 = jnp.concatenate([jnp.where(lo, kf, 0.0), jnp.where(lo, 0.0, kf)], axis=0).astype(BF16)
        p = (_dot_nt(qc, k2) * tab_scr[2]).astype(BF16)
        zb = jnp.zeros((C, LANES), BF16)
        vbd = jnp.concatenate([jnp.concatenate([vc[:, :LANES], zb], axis=1),
                               jnp.concatenate([zb, vc[:, LANES:]], axis=1)], axis=0)
        qs = (jnp.concatenate([qf, qf], axis=1) * tab_scr[0]).astype(BF16)
        o = _dot(p, vbd) + _dot(qs, st)
        halves = []
        for j in range(2):
            oh = o[:, j * LANES:(j + 1) * LANES]
            mu = jnp.mean(oh, axis=-1, keepdims=True)
            var = jnp.mean(jnp.square(oh - mu), axis=-1, keepdims=True)
            halves.append((oh - mu) * lax.rsqrt(var + EPS))
        return (jnp.concatenate(halves, axis=1) * sgc.astype(F32)).astype(BF16)

    for c in range(n_chunks):
        r = seq_rows(c)
        o_ref[r, :] = emit(q_ref[r, :], k_ref[r, :], v_ref[r, :], sg_ref[r, :], st_scr[c + 1])
    om = emit(pad_scr[0], pad_scr[1], padw_scr[0], padw_scr[1], st_scr[0])
    o_ref[seq:seq + N_META, :] = om[pad_rows:, :]


def _retention(mix, rd_lanes, batch, ltok, cast_weights, layer):
    n_chunks = (ltok - N_META) // CHUNK
    pp = RET_PAIRS_PER_STEP
    steps = RET_HEADS // 2 // pp
    narrow, wide = pp * LANES, 2 * pp * LANES
    blk = lambda cols, base: pl.BlockSpec((ltok, cols), lambda b, j: (b, base + j))
    c_in, c_out, c_shapes = _cast_riders(cast_weights, layer, batch * steps, lambda b, j: b * steps + j)
    per_pair = lambda *shape: pltpu.VMEM((pp,) + shape[:-1], shape[-1])
    return pl.pallas_call(
        functools.partial(_ret_kernel, n_chunks=n_chunks, n_cast=len(cast_weights)),
        grid=(batch, steps),
        in_specs=[
            pl.BlockSpec((pp, 8, LANES), lambda b, j: (j, 0, 0)),
            blk(narrow, OFF_QR // narrow),
            blk(narrow, OFF_KR // narrow),
            blk(wide, OFF_VR // wide),
            blk(wide, OFF_GR // wide),
        ] + c_in,
        out_specs=[pl.BlockSpec((ltok, wide), lambda b, j: (b, j))] + c_out,
        out_shape=[jax.ShapeDtypeStruct((batch * ltok, RET_V), BF16)] + c_shapes,
        scratch_shapes=[
            per_pair(3, CHUNK, 2 * LANES, F32),
            per_pair(2, CHUNK, 2 * LANES, F32),
            per_pair(2, CHUNK, LANES, BF16),
            per_pair(2, CHUNK, 2 * LANES, BF16),
            per_pair(n_chunks + 1, 2 * CHUNK, 2 * LANES, F32),
            per_pair(n_chunks + 1, 2 * CHUNK, 2 * LANES, BF16),
            per_pair(2, CHUNK, 2 * LANES, F32),
        ],
        compiler_params=pltpu.CompilerParams(dimension_semantics=("arbitrary", "arbitrary"),
                                             vmem_limit_bytes=VMEM_LIMIT),
        name="retention",
    )(rd_lanes, mix, mix, mix, mix, *cast_weights)


def _decay_lanes(ret_decay_l):
    rd = ret_decay_l.astype(F32)
    pairs = RET_HEADS // 2
    fa, fb = rd[0, 0::2], rd[0, 1::2]
    ba, bb = rd[1, 0::2], rd[1, 1::2]
    full = lambda v: jnp.broadcast_to(v[:, None], (pairs, LANES))
    split = lambda x, y: jnp.concatenate([jnp.broadcast_to(x[:, None], (pairs, RET_QK_DIM)),
                                          jnp.broadcast_to(y[:, None], (pairs, RET_QK_DIM))], axis=1)
    rows = [full(fa), full(fb), full(ba), full(bb), split(fa, fb), split(ba, bb), full(fa), full(fa)]
    return jnp.stack(rows, axis=1)


BAND = 3 * CHUNK
KEYS = BAND + CHUNK


MASK_UPPER, MASK_LOWER, MASK_METAQ, MASK_NONE, MASK_META = range(5)
MASK_ALL = None
N_MASKS = MASK_META + ATT_GROUP
SOFTMAX_ROWS = 32


def _att_kernel(*refs, n_blocks, n_cast):
    (sink_ref, q_ref, k_ref, v_ref), cast_in, o_ref, cast_out, scratch = _split_refs(refs, 4, n_cast)
    _run_cast_riders(cast_in, cast_out)
    k_scr, v_scr, *shared = scratch
    for g in range(ATT_KV_HEADS):
        cols = pl.ds(2 * g * LANES, 2 * LANES)
        own = pl.ds(g * LANES, LANES)
        _att_group(g, sink_ref, q_ref.at[:, cols], k_ref.at[:, own], v_ref.at[:, own], o_ref.at[:, cols],
                   [k_scr.at[g], v_scr.at[g]] + shared, n_blocks)


def _att_group(g, sink_ref, q_ref, k_ref, v_ref, o_ref, scratch, n_blocks):
    k_scr, v_scr, bias_scr, sa_scr, sb_scr, e_scr = scratch
    C = CHUNK
    seq = n_blocks * C
    ltok = seq + N_META
    kdup = k_ref
    vone = v_ref
    tail_row = lax.broadcasted_iota(jnp.int32, (C - N_META, LANES), 0)
    tail_lane = lax.broadcasted_iota(jnp.int32, (C - N_META, LANES), 1)
    sink_value = jnp.where((tail_row == 0) & (tail_lane >= ATT_HEAD_DIM), 1.0, 0.0).astype(BF16)
    for scr, val, tail in ((k_scr, kdup, jnp.zeros((C - N_META, LANES), BF16)), (v_scr, vone, sink_value)):
        meta_tile = jnp.concatenate([val[seq:ltok, :], tail], axis=0)
        for j in range(n_blocks):
            t = j + j // 2
            scr[t * C:(t + 1) * C, :] = val[j * C:(j + 1) * C, :]
            if j % 2 == 1:
                scr[(t + 1) * C:(t + 2) * C, :] = meta_tile

    row = lax.broadcasted_iota(jnp.int32, (C, LANES), 0)
    col = lax.broadcasted_iota(jnp.int32, (C, LANES), 1)
    visible = lambda cond: jnp.where(cond, 0.0, NEG_INF)
    bias_scr[MASK_UPPER] = visible(col >= row)
    bias_scr[MASK_LOWER] = visible(col <= row)
    bias_scr[MASK_METAQ] = visible(col <= row + (C - N_META))
    bias_scr[MASK_NONE] = visible(col < 0)
    for h in range(ATT_GROUP):
        sink = sink_ref[g * ATT_GROUP + h]
        bias_scr[MASK_META + h] = jnp.where(col < N_META, 0.0, jnp.where(col == N_META, sink, NEG_INF))

    def scores(q4, slab_start, masks, s_scr, slot):
        R = q4.shape[0]
        qf = q4.astype(F32)
        lo_r = lax.broadcasted_iota(jnp.int32, (R, LANES), 1) < ATT_HEAD_DIM
        stacked = []
        for p in range(ATT_GROUP // 2):
            qp = qf[:, p * LANES:(p + 1) * LANES]
            stacked += [jnp.where(lo_r, qp, 0.0), jnp.where(lo_r, 0.0, qp)]
        qs = jnp.concatenate(stacked, axis=0).astype(BF16)

        def bias_tile(h, mask):
            if mask is MASK_ALL:
                return jnp.zeros((R, LANES), F32)
            return bias_scr[MASK_META + h if mask == MASK_META else mask, 0:R, :]

        bias = jnp.concatenate([jnp.concatenate([bias_tile(h, mask) for mask in masks], axis=1)
                                for h in range(ATT_GROUP)], axis=0)
        s_scr[slot, 0:ATT_GROUP * R, :] = _dot_nt(qs, k_scr[pl.ds(slab_start, KEYS), :]) + bias

    def finish(R, slab_start, s_scr, slot):
        eslot = slot if s_scr is sa_scr else 2 + slot
        SR = min(R, SOFTMAX_ROWS)
        lo_r = lax.broadcasted_iota(jnp.int32, (R, LANES), 1) < ATT_HEAD_DIM
        slab = pl.ds(slab_start, KEYS)
        for r0 in range(0, ATT_GROUP * R, SR):
            rows = pl.ds(r0, SR)
            tiles = [s_scr[slot, rows, t * LANES:(t + 1) * LANES] for t in range(KEYS // LANES)]
            top = jnp.maximum(jnp.maximum(tiles[0], tiles[1]), jnp.maximum(tiles[2], tiles[3]))
            m = jnp.max(top, axis=-1, keepdims=True)
            e_scr[eslot, rows, :] = jnp.exp((s_scr[slot, rows, :] - m).astype(BF16))
        res = _dot(e_scr[eslot, 0:ATT_GROUP * R, :], v_scr[slab, :])
        outs = []
        for p in range(ATT_GROUP // 2):
            r_ev, r_od = res[2 * p * R:(2 * p + 1) * R, :], res[(2 * p + 1) * R:(2 * p + 2) * R, :]
            num = jnp.where(lo_r, r_ev, pltpu.roll(r_od, ATT_HEAD_DIM, 1))
            den = jnp.where(lo_r, pltpu.roll(r_ev, ATT_HEAD_DIM, 1), r_od)
            outs.append(num / den)
        return jnp.concatenate(outs, axis=1).astype(BF16)

    odd_masks = (MASK_UPPER, MASK_ALL, MASK_META, MASK_LOWER)
    even_masks = (MASK_UPPER, MASK_META, MASK_ALL, MASK_LOWER)
    n_pairs = n_blocks // 2 - 1

    def q_rows(n):
        return pl.ds(pl.multiple_of(n * C, C), C)

    def pair_scores(i, s_scr):
        scores(q_ref[q_rows(2 * i + 1), :], pl.multiple_of(3 * i * C, C), odd_masks, s_scr, 0)
        scores(q_ref[q_rows(2 * i + 2), :], pl.multiple_of((3 * i + 1) * C, C), even_masks, s_scr, 1)

    def pair_finish(i, s_scr):
        o_ref[q_rows(2 * i + 1), :] = finish(C, pl.multiple_of(3 * i * C, C), s_scr, 0)
        o_ref[q_rows(2 * i + 2), :] = finish(C, pl.multiple_of((3 * i + 1) * C, C), s_scr, 1)

    pair_scores(0, sa_scr)

    def body(j, carry):
        pair_scores(2 * j + 1, sb_scr)
        pair_finish(2 * j, sa_scr)
        pair_scores(2 * j + 2, sa_scr)
        pair_finish(2 * j + 1, sb_scr)
        return carry

    lax.fori_loop(0, (n_pairs - 1) // 2, body, 0)
    last = n_blocks - 1
    last_slab = ((last - 2) + (last - 2) // 2) * C
    scores(q_ref[0:C, :], 0, (MASK_ALL, MASK_LOWER, MASK_META, MASK_NONE), sb_scr, 0)
    scores(q_ref[last * C:seq, :], last_slab, (MASK_NONE, MASK_META, MASK_UPPER, MASK_ALL), sb_scr, 1)
    scores(q_ref[seq:ltok, :], 0, (MASK_METAQ, MASK_NONE, MASK_META, MASK_NONE), sb_scr, 2)
    pair_finish(n_pairs - 1, sa_scr)
    o_ref[0:C, :] = finish(C, 0, sb_scr, 0)
    o_ref[last * C:seq, :] = finish(C, last_slab, sb_scr, 1)
    o_ref[seq:ltok, :] = finish(N_META, 0, sb_scr, 2)


def _attention(mix, sink, batch, ltok, cast_weights, layer):
    n_blocks = (ltok - N_META) // CHUNK
    assert n_blocks % 4 == 0
    slab_rows = (n_blocks + n_blocks // 2) * CHUNK
    stacked = ATT_GROUP * CHUNK
    c_in, c_out, c_shapes = _cast_riders(cast_weights, layer, batch, lambda b: b)
    return pl.pallas_call(
        functools.partial(_att_kernel, n_blocks=n_blocks, n_cast=len(cast_weights)),
        grid=(batch,),
        in_specs=[
            pl.BlockSpec(memory_space=pltpu.SMEM),
            pl.BlockSpec((ltok, ATT_Q), lambda b: (b, OFF_QA // ATT_Q)),
            pl.BlockSpec((ltok, ATT_KV_HEADS * LANES), lambda b: (b, OUT_KA // (ATT_KV_HEADS * LANES))),
            pl.BlockSpec((ltok, ATT_KV_HEADS * LANES), lambda b: (b, OUT_VA // (ATT_KV_HEADS * LANES))),
        ] + c_in,
        out_specs=[pl.BlockSpec((ltok, ATT_Q), lambda b: (b, 0))] + c_out,
        out_shape=[jax.ShapeDtypeStruct((batch * ltok, ATT_Q), BF16)] + c_shapes,
        scratch_shapes=[
            pltpu.VMEM((ATT_KV_HEADS, slab_rows, LANES), BF16),
            pltpu.VMEM((ATT_KV_HEADS, slab_rows, LANES), BF16),
            pltpu.VMEM((N_MASKS, CHUNK, LANES), F32),
            pltpu.VMEM((2, stacked, KEYS), F32),
            pltpu.VMEM((3, stacked, KEYS), F32),
            pltpu.VMEM((5, stacked, KEYS), BF16),
        ],
        compiler_params=pltpu.CompilerParams(dimension_semantics=("arbitrary",), vmem_limit_bytes=VMEM_LIMIT),
        name="attention",
    )(sink, mix, mix, mix, *cast_weights)


POST_TM = 688
POST_SUB = ((0, 416), (416, 272))
FF_CHUNK = 1024


def _post_ffn_kernel(h_ref, yr_ref, ya_ref, gt_ref, wr_ref, wa_ref, wm_ref, w1_ref, w2_ref,
                     gpost_ref, gpre_ref, gffpost_ref, o_ref):
    def chain(r):
        y_r = _dot(yr_ref[r, :], wr_ref[...])
        y_a = _dot(ya_ref[r, :], wa_ref[...])
        z = gt_ref[r, :D_MODEL].astype(F32) * y_r + gt_ref[r, D_MODEL:].astype(F32) * y_a
        yield
        mix = _dot(z.astype(BF16), wm_ref[...])
        yield
        h1 = h_ref[r, :] + _rms(mix, gpost_ref[...])
        u = _rms(h1, gpre_ref[...]).astype(BF16)
        yield
        ff = jnp.zeros(h1.shape, F32)
        for c0 in range(0, D_FF, FF_CHUNK):
            hid = jnp.maximum(_dot(u, w1_ref[:, c0:c0 + FF_CHUNK]), 0.0)
            ff = ff + _dot((hid * hid).astype(BF16), w2_ref[c0:c0 + FF_CHUNK, :])
            yield
        o_ref[r, :] = h1 + _rms(ff, gffpost_ref[...])

    chains = [chain(pl.ds(start, rows)) for start, rows in POST_SUB]
    while chains:
        chains = [c for c in chains if next(c, "done") != "done"]


def _post_ffn(h, y_r, y_a, gates, wr, wa, wm, w1, w2, gpost, gpre, gffpost, layer, batch, ltok, out_rows):
    tiles = ltok // POST_TM
    tok = lambda cols: pl.BlockSpec((POST_TM, cols), lambda b, i: (b * tiles + i, 0))
    gain = _layer_resident((1, D_MODEL), layer)
    return pl.pallas_call(
        _post_ffn_kernel,
        grid=(batch, tiles),
        in_specs=[tok(D_MODEL), tok(RET_V), tok(ATT_Q), tok(GATE_COLS),
                  _resident((RET_V, D_MODEL)), _resident((ATT_Q, D_MODEL)), _resident((D_MODEL, D_MODEL)),
                  _resident((D_MODEL, D_FF)), _resident((D_FF, D_MODEL)), gain, gain, gain],
        out_specs=pl.BlockSpec((None, POST_TM, D_MODEL), lambda b, i: (b, i, 0)),
        out_shape=jax.ShapeDtypeStruct((batch, out_rows, D_MODEL), F32),
        compiler_params=pltpu.CompilerParams(dimension_semantics=("arbitrary", "arbitrary"),
                                             vmem_limit_bytes=VMEM_LIMIT),
        name="post_ffn",
    )(h, y_r, y_a, gates, wr, wa, wm, w1, w2, gpost, gpre, gffpost)


def kernel(x, meta_tokens, w_in, w_ret_o, w_att_o, w_mix_o, w_ff1, w_ff2,
           norm_mix_pre, norm_mix_post, norm_ff_pre, norm_ff_post, ret_decay, attn_sink):
    batch, seq, d = x.shape
    depth = w_in.shape[0]
    ltok = seq + N_META
    assert d == D_MODEL and seq % (2 * CHUNK) == 0 and seq >= BAND and ltok % IN_TM == 0 and ltok % POST_TM == 0
    tables = _rotary_tables(seq)
    gains = [g.astype(F32).reshape(depth, 1, d) for g in (norm_mix_pre, norm_mix_post, norm_ff_pre, norm_ff_post)]
    post_weights = tuple(w.astype(F32) for w in (w_ret_o, w_att_o, w_mix_o, w_ff1, w_ff2))
    w_in_f = w_in.astype(F32)
    w_in_b = w_in_f[0].astype(BF16)
    for l in range(depth):
        last = l == depth - 1
        if l == 0:
            mix, gates, h = _in_proj(x.astype(F32), gains[0], w_in_b, tables, l, batch, ltok,
                                     meta=meta_tokens.astype(F32))
        else:
            mix, gates = _in_proj(h, gains[0], w_in_b, tables, l, batch, ltok)
        y_r, *post_b = _retention(mix, _decay_lanes(ret_decay[l]), batch, ltok, post_weights, l)
        y_a, *next_in = _attention(mix, attn_sink[l].astype(F32), batch, ltok, () if last else (w_in_f,), l + 1)
        if not last:
            w_in_b, = next_in
        h = _post_ffn(h, y_r, y_a, gates, *post_b,
                      gains[1], gains[2], gains[3], l, batch, ltok, seq if last else ltok)
        if not last:
            h = h.reshape(batch * ltok, d)
    return h
```

```python
import functools

import jax
import jax.numpy as jnp
from jax import lax
from jax.experimental import pallas as pl
from jax.experimental.pallas import tpu as pltpu

F32 = jnp.float32
BF16 = jnp.bfloat16

D_MODEL = 1024
N_META = 16
CHUNK = 128
RET_HEADS = 8
RET_QK_DIM = 64
RET_V_DIM = 128
ATT_Q_HEADS = 8
ATT_KV_HEADS = 2
ATT_GROUP = ATT_Q_HEADS // ATT_KV_HEADS
ATT_HEAD_DIM = 64
ROPE_DIM = 16
ROPE_THETA = 500000.0
XPOS_THETA = 10000.0
D_FF = 4 * D_MODEL
EPS = 1e-6
NEG_INF = -1e30

RET_QK = RET_HEADS * RET_QK_DIM
RET_V = RET_HEADS * RET_V_DIM
ATT_Q = ATT_Q_HEADS * ATT_HEAD_DIM
ATT_KV = ATT_KV_HEADS * ATT_HEAD_DIM
OFF_QR = 0
OFF_KR = OFF_QR + RET_QK
OFF_VR = OFF_KR + RET_QK
OFF_GR = OFF_VR + RET_V
OFF_QA = OFF_GR + RET_V
OFF_KA = OFF_QA + ATT_Q
OFF_VA = OFF_KA + ATT_KV
MIX_COLS = OFF_VA + ATT_KV
GATE_COLS = 2 * D_MODEL
D_IN = MIX_COLS + GATE_COLS
OUT_KA = OFF_KA
OUT_VA = OUT_KA + ATT_KV_HEADS * 2 * ATT_HEAD_DIM
OUT_COLS = OUT_VA + ATT_KV_HEADS * 2 * ATT_HEAD_DIM

LANES = 128
MXU_N = 256
VMEM_LIMIT = 60 * 1024 * 1024


def _dot(a, b):
    return jnp.dot(a, b, preferred_element_type=F32)


def _dot_nt(a, b):
    return lax.dot_general(a, b, (((1,), (1,)), ((), ())), preferred_element_type=F32)


def _rms(x, g):
    return x * lax.rsqrt(jnp.mean(x * x, axis=-1, keepdims=True) + EPS) * g


def _sigmoid(x):
    return 1.0 / (1.0 + jnp.exp(-x))


def _resident(shape):
    return pl.BlockSpec(shape, lambda *_: (0,) * len(shape), pipeline_mode=pl.Buffered(1))


def _cast_riders(weights, layer, steps, step_index):
    in_specs, out_specs, out_shapes = [], [], []
    for w in weights:
        _, k, n = w.shape
        rows = k // steps
        assert rows * steps == k and rows % 16 == 0, (k, steps)
        in_specs.append(pl.BlockSpec((None, rows, n), lambda *g: (layer, step_index(*g), 0)))
        out_specs.append(pl.BlockSpec((rows, n), lambda *g: (step_index(*g), 0)))
        out_shapes.append(jax.ShapeDtypeStruct((k, n), BF16))
    return in_specs, out_specs, out_shapes


def _split_refs(refs, n_in, n_cast):
    a, b = n_in + n_cast, n_in + n_cast + 1
    return refs[:n_in], refs[n_in:a], refs[a], refs[b:b + n_cast], refs[b + n_cast:]


def _run_cast_riders(cast_in, cast_out):
    for src, dst in zip(cast_in, cast_out):
        dst[...] = src[...].astype(BF16)


def _layer_resident(shape, layer):
    return pl.BlockSpec((None,) + shape, lambda *_: (layer,) + (0,) * len(shape), pipeline_mode=pl.Buffered(1))


def _rotary_tables(seq):
    pos = jnp.concatenate([jnp.arange(N_META, N_META + seq), jnp.arange(N_META)]).astype(F32)
    d = jnp.arange(LANES) % RET_QK_DIM

    def table(theta, rot_dim):
        half = rot_dim // 2
        freqs = jnp.power(jnp.float32(theta), -jnp.arange(0, rot_dim, 2, dtype=F32) / rot_dim)
        ang = pos[:, None] * freqs[None, :]
        cos, sin = jnp.cos(ang), jnp.sin(ang)
        idx = d % half
        rotated = (d < rot_dim)[None, :]
        cos_l = jnp.where(rotated, cos[:, idx], 1.0)
        sin_l = jnp.where(rotated, jnp.where((d < half)[None, :], -sin[:, idx], sin[:, idx]), 0.0)
        return cos_l, sin_l

    cr, sr = table(XPOS_THETA, RET_QK_DIM)
    ca, sa = table(ROPE_THETA, ROPE_DIM)
    qs_r = RET_QK_DIM ** -0.5
    qs_a = ATT_HEAD_DIM ** -0.5
    return (cr * qs_r, sr * qs_r, cr, sr, ca * qs_a, sa * qs_a, ca, sa)


def _rotate(x, cos, sin, half):
    lane = lax.broadcasted_iota(jnp.int32, x.shape, 1) % RET_QK_DIM
    partner = jnp.where(lane < half, pltpu.roll(x, LANES - half, 1), pltpu.roll(x, half, 1))
    return x * cos + partner * sin


IN_TM = 688
IN_SUB = ((0, 272), (272, 416))


def _in_proj_kernel(*refs, assemble):
    h_ref, g_ref, w_ref, crq, srq, crk, srk, caq, saq, cak, sak = refs[:11]
    if assemble:
        meta_ref, mix_ref, gate_ref, h0_ref = refs[11:]
    else:
        mix_ref, gate_ref = refs[11:]

    def chain(start, rows):
        r = pl.ds(start, rows)
        x = h_ref[r, :]
        if assemble:
            if start + rows == IN_TM:
                last_tile = pl.program_id(1) == pl.num_programs(1) - 1
                row = lax.broadcasted_iota(jnp.int32, (rows, D_MODEL), 0)
                meta_rows = jnp.concatenate([jnp.zeros((rows - N_META, D_MODEL), F32), meta_ref[...]], axis=0)
                x = jnp.where((row >= rows - N_META) & last_tile, meta_rows, x)
            h0_ref[r, :] = x
        u = (x * g_ref[...]).astype(BF16)
        inv = jnp.broadcast_to(lax.rsqrt(jnp.mean(x * x, axis=-1, keepdims=True) + EPS), (rows, LANES))
        yield

        def proj(c0, n):
            return _dot(u, w_ref[:, c0:c0 + n]) * jnp.concatenate([inv] * (n // LANES), axis=1)

        def rot_store(c0, cos_ref, sin_ref, half):
            p = proj(c0, MXU_N)
            for j in range(MXU_N // LANES):
                t = _rotate(p[:, j * LANES:(j + 1) * LANES], cos_ref[r, :], sin_ref[r, :], half)
                mix_ref[r, c0 + j * LANES:c0 + (j + 1) * LANES] = t.astype(BF16)

        for c0 in range(OFF_QR, OFF_KR, MXU_N):
            rot_store(c0, crq, srq, RET_QK_DIM // 2)
            yield
        for c0 in range(OFF_KR, OFF_VR, MXU_N):
            rot_store(c0, crk, srk, RET_QK_DIM // 2)
            yield
        for c0 in range(OFF_GR, OFF_QA, MXU_N):
            p = proj(c0, MXU_N)
            mix_ref[r, c0:c0 + MXU_N] = (p * _sigmoid(p)).astype(BF16)
            yield
        for c0 in range(OFF_QA, OFF_KA, MXU_N):
            rot_store(c0, caq, saq, ROPE_DIM // 2)
            yield
        p = proj(OFF_KA, MXU_N)
        k_heads = _rotate(p[:, :LANES], cak[r, :], sak[r, :], ROPE_DIM // 2)
        v_heads = p[:, LANES:]
        low = lax.broadcasted_iota(jnp.int32, (rows, LANES), 1) < ATT_HEAD_DIM
        k_swapped = pltpu.roll(k_heads, ATT_HEAD_DIM, 1)
        v_swapped = pltpu.roll(v_heads, ATT_HEAD_DIM, 1)
        for g, (kg, vg) in enumerate(((jnp.where(low, k_heads, k_swapped), jnp.where(low, v_heads, 1.0)),
                                      (jnp.where(low, k_swapped, k_heads), jnp.where(low, v_swapped, 1.0)))):
            mix_ref[r, OUT_KA + g * LANES:OUT_KA + (g + 1) * LANES] = kg.astype(BF16)
            mix_ref[r, OUT_VA + g * LANES:OUT_VA + (g + 1) * LANES] = vg.astype(BF16)
        yield
        for c0 in range(0, GATE_COLS, MXU_N):
            gate_ref[r, c0:c0 + MXU_N] = _sigmoid(proj(MIX_COLS + c0, MXU_N)).astype(BF16)
            yield
        for c0 in range(OFF_VR, OFF_GR, MXU_N):
            mix_ref[r, c0:c0 + MXU_N] = proj(c0, MXU_N).astype(BF16)
            yield

    chains = [chain(start, rows) for start, rows in IN_SUB]
    while chains:
        chains = [c for c in chains if next(c, "done") != "done"]


def _in_proj(h, gains, w, tables, layer, batch, ltok, meta=None):
    assemble = meta is not None
    tiles = ltok // IN_TM
    t = batch * ltok
    tok = lambda cols: pl.BlockSpec((IN_TM, cols), lambda b, i: (b * tiles + i, 0))
    tab = pl.BlockSpec((IN_TM, LANES), lambda b, i: (i, 0))
    h_spec = pl.BlockSpec((None, IN_TM, D_MODEL), lambda b, i: (b, i, 0)) if assemble else tok(D_MODEL)
    in_specs = [h_spec, _layer_resident((1, D_MODEL), layer), _resident((D_MODEL, D_IN))] + [tab] * 8
    out_specs = [tok(OUT_COLS), tok(GATE_COLS)]
    out_shape = [jax.ShapeDtypeStruct((t, OUT_COLS), BF16), jax.ShapeDtypeStruct((t, GATE_COLS), BF16)]
    operands = [h, gains, w, *tables]
    if assemble:
        in_specs.append(_resident((N_META, D_MODEL)))
        out_specs.append(tok(D_MODEL))
        out_shape.append(jax.ShapeDtypeStruct((t, D_MODEL), F32))
        operands.append(meta)
    return pl.pallas_call(
        functools.partial(_in_proj_kernel, assemble=assemble),
        grid=(batch, tiles),
        in_specs=in_specs,
        out_specs=out_specs,
        out_shape=out_shape,
        compiler_params=pltpu.CompilerParams(dimension_semantics=("arbitrary", "arbitrary"),
                                             vmem_limit_bytes=VMEM_LIMIT),
        name="in_proj",
    )(*operands)


RET_PAIRS_PER_STEP = 2


def _ret_kernel(*refs, n_chunks, n_cast):
    (rd_ref, q_ref, k_ref, v_ref, sg_ref), cast_in, o_ref, cast_out, scratch = _split_refs(refs, 5, n_cast)
    _run_cast_riders(cast_in, cast_out)
    for p in range(RET_PAIRS_PER_STEP):
        narrow, wide = pl.ds(p * LANES, LANES), pl.ds(2 * p * LANES, 2 * LANES)
        _ret_pair(rd_ref.at[p], q_ref.at[:, narrow], k_ref.at[:, narrow], v_ref.at[:, wide], sg_ref.at[:, wide],
                  o_ref.at[:, wide], [s.at[p] for s in scratch], n_chunks)


def _ret_pair(rd_ref, q_ref, k_ref, v_ref, sg_ref, o_ref, scratch, n_chunks):
    tab_scr, dec_scr, pad_scr, padw_scr, kv_scr, st_scr, s_scr = scratch
    C = CHUNK
    seq = n_chunks * C
    W = 2 * LANES
    H = RET_QK_DIM
    lg = -jnp.exp(rd_ref[...])
    lgf_a, lgf_b, lgb_a, lgb_b, lgf_p, lgb_p = [lg[i:i + 1, :] for i in range(6)]

    row = lax.broadcasted_iota(jnp.int32, (C, LANES), 0)
    col = lax.broadcasted_iota(jnp.int32, (C, LANES), 1)
    a = row.astype(F32)
    rel = (row - col).astype(F32)

    def decay_matrix(lf, lb):
        return jnp.where(rel >= 0, jnp.exp(lf * jnp.maximum(rel, 0.0)), jnp.exp(lb * jnp.maximum(-rel, 0.0)))

    tab_scr[0] = jnp.concatenate([jnp.exp(lgf_p * (a + 1.0)), jnp.exp(lgb_p * (C - a))], axis=1)
    tab_scr[1] = jnp.concatenate([jnp.exp(lgf_p * (C - 1.0 - a)), jnp.exp(lgb_p * a)], axis=1)
    tab_scr[2] = jnp.concatenate([decay_matrix(lgf_a, lgb_a), decay_matrix(lgf_b, lgb_b)], axis=1)
    top = row < H
    for i, (la, lb) in enumerate(((lgf_a, lgf_b), (lgb_a, lgb_b))):
        g = jnp.where(top, jnp.exp(la * C), jnp.exp(lb * C))
        dec_scr[i] = jnp.concatenate([g, g], axis=1)
    lo = col < H

    pad_rows = C - N_META
    for i, ref in enumerate((q_ref, k_ref)):
        pad_scr[i, 0:pad_rows, :] = jnp.zeros((pad_rows, LANES), BF16)
        pad_scr[i, pad_rows:C, :] = ref[seq:seq + N_META, :]
    for i, ref in enumerate((v_ref, sg_ref)):
        padw_scr[i, 0:pad_rows, :] = jnp.zeros((pad_rows, W), BF16)
        padw_scr[i, pad_rows:C, :] = ref[seq:seq + N_META, :]

    def seq_rows(c):
        return pl.ds(c * C, C)

    def kv(kc, vc):
        kf = kc.astype(F32)
        ks = jnp.concatenate([kf, kf], axis=1) * tab_scr[1]
        kt = jnp.concatenate([ks[:, :LANES].T, ks[:, LANES:].T], axis=0).astype(BF16)
        x = _dot(kt, vc)
        z = jnp.zeros((H, LANES), F32)
        rows = []
        for d in range(2):
            rows.append(jnp.concatenate([x[d * C:d * C + H, :LANES], z], axis=1))
            rows.append(jnp.concatenate([z, x[d * C + H:(d + 1) * C, LANES:]], axis=1))
        return jnp.concatenate(rows, axis=0)

    kv_scr[0] = kv(pad_scr[1], padw_scr[0])
    for c in range(n_chunks):
        r = seq_rows(c)
        kv_scr[c + 1] = kv(k_ref[r, :], v_ref[r, :])

    s_scr[...] = jnp.zeros((2, C, W), F32)
    for t in range(n_chunks + 1):
        u = n_chunks - t
        sf = s_scr[0]
        sb = s_scr[1]
        st_scr[t, 0:C, :] = sf.astype(BF16)
        st_scr[u, C:2 * C, :] = sb.astype(BF16)
        s_scr[0] = sf * dec_scr[0] + kv_scr[t, 0:C, :]
        s_scr[1] = sb * dec_scr[1] + kv_scr[u, C:2 * C, :]

    def emit(qc, kc, vc, sgc, st):
        qf = qc.astype(F32)
        kf = kc.astype(F32)
        k2 = jnp.concatenate([jnp.where(lo, kf, 0.0), jnp.where(lo, 0.0, kf)], axis=0).astype(BF16)
        p = (_dot_nt(qc, k2) * tab_scr[2]).astype(BF16)
        zb = jnp.zeros((C, LANES), BF16)
        vbd = jnp.concatenate([jnp.concatenate([vc[:, :LANES], zb], axis=1),
                               jnp.concatenate([zb, vc[:, LANES:]], axis=1)], axis=0)
        qs = (jnp.concatenate([qf, qf], axis=1) * tab_scr[0]).astype(BF16)
        o = _dot(p, vbd) + _dot(qs, st)
        halves = []
        for j in range(2):
            oh = o[:, j * LANES:(j + 1) * LANES]
            mu = jnp.mean(oh, axis=-1, keepdims=True)
            var = jnp.mean(jnp.square(oh - mu), axis=-1, keepdims=True)
            halves.append((oh - mu) * lax.rsqrt(var + EPS))
        return (jnp.concatenate(halves, axis=1) * sgc.astype(F32)).astype(BF16)

    for c in range(n_chunks):
        r = seq_rows(c)
        o_ref[r, :] = emit(q_ref[r, :], k_ref[r, :], v_ref[r, :], sg_ref[r, :], st_scr[c + 1])
    om = emit(pad_scr[0], pad_scr[1], padw_scr[0], padw_scr[1], st_scr[0])
    o_ref[seq:seq + N_META, :] = om[pad_rows:, :]


def _retention(mix, rd_lanes, batch, ltok, cast_weights, layer):
    n_chunks = (ltok - N_META) // CHUNK
    pp = RET_PAIRS_PER_STEP
    steps = RET_HEADS // 2 // pp
    narrow, wide = pp * LANES, 2 * pp * LANES
    blk = lambda cols, base: pl.BlockSpec((ltok, cols), lambda b, j: (b, base + j))
    c_in, c_out, c_shapes = _cast_riders(cast_weights, layer, batch * steps, lambda b, j: b * steps + j)
    per_pair = lambda *shape: pltpu.VMEM((pp,) + shape[:-1], shape[-1])
    return pl.pallas_call(
        functools.partial(_ret_kernel, n_chunks=n_chunks, n_cast=len(cast_weights)),
        grid=(batch, steps),
        in_specs=[
            pl.BlockSpec((pp, 8, LANES), lambda b, j: (j, 0, 0)),
            blk(narrow, OFF_QR // narrow),
            blk(narrow, OFF_KR // narrow),
            blk(wide, OFF_VR // wide),
            blk(wide, OFF_GR // wide),
        ] + c_in,
        out_specs=[pl.BlockSpec((ltok, wide), lambda b, j: (b, j))] + c_out,
        out_shape=[jax.ShapeDtypeStruct((batch * ltok, RET_V), BF16)] + c_shapes,
        scratch_shapes=[
            per_pair(3, CHUNK, 2 * LANES, F32),
            per_pair(2, CHUNK, 2 * LANES, F32),
            per_pair(2, CHUNK, LANES, BF16),
            per_pair(2, CHUNK, 2 * LANES, BF16),
            per_pair(n_chunks + 1, 2 * CHUNK, 2 * LANES, F32),
            per_pair(n_chunks + 1, 2 * CHUNK, 2 * LANES, BF16),
            per_pair(2, CHUNK, 2 * LANES, F32),
        ],
        compiler_params=pltpu.CompilerParams(dimension_semantics=("arbitrary", "arbitrary"),
                                             vmem_limit_bytes=VMEM_LIMIT),
        name="retention",
    )(rd_lanes, mix, mix, mix, mix, *cast_weights)


def _decay_lanes(ret_decay_l):
    rd = ret_decay_l.astype(F32)
    pairs = RET_HEADS // 2
    fa, fb = rd[0, 0::2], rd[0, 1::2]
    ba, bb = rd[1, 0::2], rd[1, 1::2]
    full = lambda v: jnp.broadcast_to(v[:, None], (pairs, LANES))
    split = lambda x, y: jnp.concatenate([jnp.broadcast_to(x[:, None], (pairs, RET_QK_DIM)),
                                          jnp.broadcast_to(y[:, None], (pairs, RET_QK_DIM))], axis=1)
    rows = [full(fa), full(fb), full(ba), full(bb), split(fa, fb), split(ba, bb), full(fa), full(fa)]
    return jnp.stack(rows, axis=1)


BAND = 3 * CHUNK
KEYS = BAND + CHUNK


MASK_UPPER, MASK_LOWER, MASK_METAQ, MASK_NONE, MASK_META = range(5)
MASK_ALL = None
N_MASKS = MASK_META + ATT_GROUP
SOFTMAX_ROWS = 32


def _att_kernel(*refs, n_blocks, n_cast):
    (sink_ref, q_ref, k_ref, v_ref), cast_in, o_ref, cast_out, scratch = _split_refs(refs, 4, n_cast)
    _run_cast_riders(cast_in, cast_out)
    k_scr, v_scr, *shared = scratch
    for g in range(ATT_KV_HEADS):
        cols = pl.ds(2 * g * LANES, 2 * LANES)
        own = pl.ds(g * LANES, LANES)
        _att_group(g, sink_ref, q_ref.at[:, cols], k_ref.at[:, own], v_ref.at[:, own], o_ref.at[:, cols],
                   [k_scr.at[g], v_scr.at[g]] + shared, n_blocks)


def _att_group(g, sink_ref, q_ref, k_ref, v_ref, o_ref, scratch, n_blocks):
    k_scr, v_scr, bias_scr, sa_scr, sb_scr, e_scr = scratch
    C = CHUNK
    seq = n_blocks * C
    ltok = seq + N_META
    kdup = k_ref
    vone = v_ref
    tail_row = lax.broadcasted_iota(jnp.int32, (C - N_META, LANES), 0)
    tail_lane = lax.broadcasted_iota(jnp.int32, (C - N_META, LANES), 1)
    sink_value = jnp.where((tail_row == 0) & (tail_lane >= ATT_HEAD_DIM), 1.0, 0.0).astype(BF16)
    for scr, val, tail in ((k_scr, kdup, jnp.zeros((C - N_META, LANES), BF16)), (v_scr, vone, sink_value)):
        meta_tile = jnp.concatenate([val[seq:ltok, :], tail], axis=0)
        for j in range(n_blocks):
            t = j + j // 2
            scr[t * C:(t + 1) * C, :] = val[j * C:(j + 1) * C, :]
            if j % 2 == 1:
                scr[(t + 1) * C:(t + 2) * C, :] = meta_tile

    row = lax.broadcasted_iota(jnp.int32, (C, LANES), 0)
    col = lax.broadcasted_iota(jnp.int32, (C, LANES), 1)
    visible = lambda cond: jnp.where(cond, 0.0, NEG_INF)
    bias_scr[MASK_UPPER] = visible(col >= row)
    bias_scr[MASK_LOWER] = visible(col <= row)
    bias_scr[MASK_METAQ] = visible(col <= row + (C - N_META))
    bias_scr[MASK_NONE] = visible(col < 0)
    for h in range(ATT_GROUP):
        sink = sink_ref[g * ATT_GROUP + h]
        bias_scr[MASK_META + h] = jnp.where(col < N_META, 0.0, jnp.where(col == N_META, sink, NEG_INF))

    def scores(q4, slab_start, masks, s_scr, slot):
        R = q4.shape[0]
        qf = q4.astype(F32)
        lo_r = lax.broadcasted_iota(jnp.int32, (R, LANES), 1) < ATT_HEAD_DIM
        stacked = []
        for p in range(ATT_GROUP // 2):
            qp = qf[:, p * LANES:(p + 1) * LANES]
            stacked += [jnp.where(lo_r, qp, 0.0), jnp.where(lo_r, 0.0, qp)]
        qs = jnp.concatenate(stacked, axis=0).astype(BF16)

        def bias_tile(h, mask):
            if mask is MASK_ALL:
                return jnp.zeros((R, LANES), F32)
            return bias_scr[MASK_META + h if mask == MASK_META else mask, 0:R, :]

        bias = jnp.concatenate([jnp.concatenate([bias_tile(h, mask) for mask in masks], axis=1)
                                for h in range(ATT_GROUP)], axis=0)
        s_scr[slot, 0:ATT_GROUP * R, :] = _dot_nt(qs, k_scr[pl.ds(slab_start, KEYS), :]) + bias

    def finish(R, slab_start, s_scr, slot):
        eslot = slot if s_scr is sa_scr else 2 + slot
        SR = min(R, SOFTMAX_ROWS)
        lo_r = lax.broadcasted_iota(jnp.int32, (R, LANES), 1) < ATT_HEAD_DIM
        slab = pl.ds(slab_start, KEYS)
        for r0 in range(0, ATT_GROUP * R, SR):
            rows = pl.ds(r0, SR)
            tiles = [s_scr[slot, rows, t * LANES:(t + 1) * LANES] for t in range(KEYS // LANES)]
            top = jnp.maximum(jnp.maximum(tiles[0], tiles[1]), jnp.maximum(tiles[2], tiles[3]))
            m = jnp.max(top, axis=-1, keepdims=True)
            e_scr[eslot, rows, :] = jnp.exp((s_scr[slot, rows, :] - m).astype(BF16))
        res = _dot(e_scr[eslot, 0:ATT_GROUP * R, :], v_scr[slab, :])
        outs = []
        for p in range(ATT_GROUP // 2):
            r_ev, r_od = res[2 * p * R:(2 * p + 1) * R, :], res[(2 * p + 1) * R:(2 * p + 2) * R, :]
            num = jnp.where(lo_r, r_ev, pltpu.roll(r_od, ATT_HEAD_DIM, 1))
            den = jnp.where(lo_r, pltpu.roll(r_ev, ATT_HEAD_DIM, 1), r_od)
            outs.append(num / den)
        return jnp.concatenate(outs, axis=1).astype(BF16)

    odd_masks = (MASK_UPPER, MASK_ALL, MASK_META, MASK_LOWER)
    even_masks = (MASK_UPPER, MASK_META, MASK_ALL, MASK_LOWER)
    n_pairs = n_blocks // 2 - 1

    def q_rows(n):
        return pl.ds(pl.multiple_of(n * C, C), C)

    def pair_scores(i, s_scr):
        scores(q_ref[q_rows(2 * i + 1), :], pl.multiple_of(3 * i * C, C), odd_masks, s_scr, 0)
        scores(q_ref[q_rows(2 * i + 2), :], pl.multiple_of((3 * i + 1) * C, C), even_masks, s_scr, 1)

    def pair_finish(i, s_scr):
        o_ref[q_rows(2 * i + 1), :] = finish(C, pl.multiple_of(3 * i * C, C), s_scr, 0)
        o_ref[q_rows(2 * i + 2), :] = finish(C, pl.multiple_of((3 * i + 1) * C, C), s_scr, 1)

    pair_scores(0, sa_scr)

    def body(j, carry):
        pair_scores(2 * j + 1, sb_scr)
        pair_finish(2 * j, sa_scr)
        pair_scores(2 * j + 2, sa_scr)
        pair_finish(2 * j + 1, sb_scr)
        return carry

    lax.fori_loop(0, (n_pairs - 1) // 2, body, 0)
    last = n_blocks - 1
    last_slab = ((last - 2) + (last - 2) // 2) * C
    scores(q_ref[0:C, :], 0, (MASK_ALL, MASK_LOWER, MASK_META, MASK_NONE), sb_scr, 0)
    scores(q_ref[last * C:seq, :], last_slab, (MASK_NONE, MASK_META, MASK_UPPER, MASK_ALL), sb_scr, 1)
    scores(q_ref[seq:ltok, :], 0, (MASK_METAQ, MASK_NONE, MASK_META, MASK_NONE), sb_scr, 2)
    pair_finish(n_pairs - 1, sa_scr)
    o_ref[0:C, :] = finish(C, 0, sb_scr, 0)
    o_ref[last * C:seq, :] = finish(C, last_slab, sb_scr, 1)
    o_ref[seq:ltok, :] = finish(N_META, 0, sb_scr, 2)


def _attention(mix, sink, batch, ltok, cast_weights, layer):
    n_blocks = (ltok - N_META) // CHUNK
    assert n_blocks % 4 == 0
    slab_rows = (n_blocks + n_blocks // 2) * CHUNK
    stacked = ATT_GROUP * CHUNK
    c_in, c_out, c_shapes = _cast_riders(cast_weights, layer, batch, lambda b: b)
    return pl.pallas_call(
        functools.partial(_att_kernel, n_blocks=n_blocks, n_cast=len(cast_weights)),
        grid=(batch,),
        in_specs=[
            pl.BlockSpec(memory_space=pltpu.SMEM),
            pl.BlockSpec((ltok, ATT_Q), lambda b: (b, OFF_QA // ATT_Q)),
            pl.BlockSpec((ltok, ATT_KV_HEADS * LANES), lambda b: (b, OUT_KA // (ATT_KV_HEADS * LANES))),
            pl.BlockSpec((ltok, ATT_KV_HEADS * LANES), lambda b: (b, OUT_VA // (ATT_KV_HEADS * LANES))),
        ] + c_in,
        out_specs=[pl.BlockSpec((ltok, ATT_Q), lambda b: (b, 0))] + c_out,
        out_shape=[jax.ShapeDtypeStruct((batch * ltok, ATT_Q), BF16)] + c_shapes,
        scratch_shapes=[
            pltpu.VMEM((ATT_KV_HEADS, slab_rows, LANES), BF16),
            pltpu.VMEM((ATT_KV_HEADS, slab_rows, LANES), BF16),
            pltpu.VMEM((N_MASKS, CHUNK, LANES), F32),
            pltpu.VMEM((2, stacked, KEYS), F32),
            pltpu.VMEM((3, stacked, KEYS), F32),
            pltpu.VMEM((5, stacked, KEYS), BF16),
        ],
        compiler_params=pltpu.CompilerParams(dimension_semantics=("arbitrary",), vmem_limit_bytes=VMEM_LIMIT),
        name="attention",
    )(sink, mix, mix, mix, *cast_weights)


def _mixer_kernel(*refs, n_chunks, n_cast):
    rd_ref, q_ref, k_ref, v_ref, sg_ref, sink_ref, qa_ref, ka_ref, va_ref = refs[:9]
    cast_in = refs[9:9 + n_cast]
    yr_ref, ya_ref = refs[9 + n_cast:11 + n_cast]
    cast_out = refs[11 + n_cast:11 + 2 * n_cast]
    scratch = refs[11 + 2 * n_cast:]
    ret_scratch, att_scratch = scratch[:7], list(scratch[7:])
    _run_cast_riders(cast_in, cast_out)
    for p in range(RET_PAIRS_PER_STEP):
        narrow, wide = pl.ds(p * LANES, LANES), pl.ds(2 * p * LANES, 2 * LANES)
        _ret_pair(rd_ref.at[p], q_ref.at[:, narrow], k_ref.at[:, narrow], v_ref.at[:, wide], sg_ref.at[:, wide],
                  yr_ref.at[:, wide], [s.at[p] for s in ret_scratch], n_chunks)
    _att_group(pl.program_id(1), sink_ref, qa_ref, ka_ref, va_ref, ya_ref, att_scratch, n_chunks)


def _mixers(mix, rd_lanes, sink, batch, ltok, casts):
    n_chunks = (ltok - N_META) // CHUNK
    assert n_chunks % 4 == 0
    pp = RET_PAIRS_PER_STEP
    steps = RET_HEADS // 2 // pp
    assert steps == ATT_KV_HEADS
    narrow, wide = pp * LANES, 2 * pp * LANES
    slab_rows = (n_chunks + n_chunks // 2) * CHUNK
    stacked = ATT_GROUP * CHUNK
    blk = lambda cols, base: pl.BlockSpec((ltok, cols), lambda b, j: (b, base + j))
    c_in, c_out, c_shapes, c_ops = [], [], [], []
    for weights, layer in casts:
        ci, co, cs = _cast_riders(weights, layer, batch * steps, lambda b, j: b * steps + j)
        c_in += ci
        c_out += co
        c_shapes += cs
        c_ops += list(weights)
    per_pair = lambda *shape: pltpu.VMEM((pp,) + shape[:-1], shape[-1])
    return pl.pallas_call(
        functools.partial(_mixer_kernel, n_chunks=n_chunks, n_cast=len(c_ops)),
        grid=(batch, steps),
        in_specs=[
            pl.BlockSpec((pp, 8, LANES), lambda b, j: (j, 0, 0)),
            blk(narrow, OFF_QR // narrow),
            blk(narrow, OFF_KR // narrow),
            blk(wide, OFF_VR // wide),
            blk(wide, OFF_GR // wide),
            pl.BlockSpec(memory_space=pltpu.SMEM),
            blk(2 * LANES, OFF_QA // (2 * LANES)),
            blk(LANES, OUT_KA // LANES),
            blk(LANES, OUT_VA // LANES),
        ] + c_in,
        out_specs=[pl.BlockSpec((ltok, wide), lambda b, j: (b, j)),
                   pl.BlockSpec((ltok, 2 * LANES), lambda b, j: (b, j))] + c_out,
        out_shape=[jax.ShapeDtypeStruct((batch * ltok, RET_V), BF16),
                   jax.ShapeDtypeStruct((batch * ltok, ATT_Q), BF16)] + c_shapes,
        scratch_shapes=[
            per_pair(3, CHUNK, 2 * LANES, F32),
            per_pair(2, CHUNK, 2 * LANES, F32),
            per_pair(2, CHUNK, LANES, BF16),
            per_pair(2, CHUNK, 2 * LANES, BF16),
            per_pair(n_chunks + 1, 2 * CHUNK, 2 * LANES, F32),
            per_pair(n_chunks + 1, 2 * CHUNK, 2 * LANES, BF16),
            per_pair(2, CHUNK, 2 * LANES, F32),
            pltpu.VMEM((slab_rows, LANES), BF16),
            pltpu.VMEM((slab_rows, LANES), BF16),
            pltpu.VMEM((N_MASKS, CHUNK, LANES), F32),
            pltpu.VMEM((2, stacked, KEYS), F32),
            pltpu.VMEM((3, stacked, KEYS), F32),
            pltpu.VMEM((5, stacked, KEYS), BF16),
        ],
        compiler_params=pltpu.CompilerParams(dimension_semantics=("arbitrary", "arbitrary"),
                                             vmem_limit_bytes=VMEM_LIMIT),
        name="mixers",
    )(rd_lanes, mix, mix, mix, mix, sink, mix, mix, mix, *c_ops)


POST_TM = 688
POST_SUB = ((0, 416), (416, 272))
FF_CHUNK = 1024


def _post_ffn_kernel(h_ref, yr_ref, ya_ref, gt_ref, wr_ref, wa_ref, wm_ref, w1_ref, w2_ref,
                     gpost_ref, gpre_ref, gffpost_ref, o_ref):
    def chain(r):
        y_r = _dot(yr_ref[r, :], wr_ref[...])
        y_a = _dot(ya_ref[r, :], wa_ref[...])
        z = gt_ref[r, :D_MODEL].astype(F32) * y_r + gt_ref[r, D_MODEL:].astype(F32) * y_a
        yield
        mix = _dot(z.astype(BF16), wm_ref[...])
        yield
        h1 = h_ref[r, :] + _rms(mix, gpost_ref[...])
        u = _rms(h1, gpre_ref[...]).astype(BF16)
        yield
        ff = jnp.zeros(h1.shape, F32)
        for c0 in range(0, D_FF, FF_CHUNK):
            hid = jnp.maximum(_dot(u, w1_ref[:, c0:c0 + FF_CHUNK]), 0.0)
            ff = ff + _dot((hid * hid).astype(BF16), w2_ref[c0:c0 + FF_CHUNK, :])
            yield
        o_ref[r, :] = h1 + _rms(ff, gffpost_ref[...])

    chains = [chain(pl.ds(start, rows)) for start, rows in POST_SUB]
    while chains:
        chains = [c for c in chains if next(c, "done") != "done"]


def _post_ffn(h, y_r, y_a, gates, wr, wa, wm, w1, w2, gpost, gpre, gffpost, layer, batch, ltok, out_rows):
    tiles = ltok // POST_TM
    tok = lambda cols: pl.BlockSpec((POST_TM, cols), lambda b, i: (b * tiles + i, 0))
    gain = _layer_resident((1, D_MODEL), layer)
    return pl.pallas_call(
        _post_ffn_kernel,
        grid=(batch, tiles),
        in_specs=[tok(D_MODEL), tok(RET_V), tok(ATT_Q), tok(GATE_COLS),
                  _resident((RET_V, D_MODEL)), _resident((ATT_Q, D_MODEL)), _resident((D_MODEL, D_MODEL)),
                  _resident((D_MODEL, D_FF)), _resident((D_FF, D_MODEL)), gain, gain, gain],
        out_specs=pl.BlockSpec((None, POST_TM, D_MODEL), lambda b, i: (b, i, 0)),
        out_shape=jax.ShapeDtypeStruct((batch, out_rows, D_MODEL), F32),
        compiler_params=pltpu.CompilerParams(dimension_semantics=("arbitrary", "arbitrary"),
                                             vmem_limit_bytes=VMEM_LIMIT),
        name="post_ffn",
    )(h, y_r, y_a, gates, wr, wa, wm, w1, w2, gpost, gpre, gffpost)


def kernel(x, meta_tokens, w_in, w_ret_o, w_att_o, w_mix_o, w_ff1, w_ff2,
           norm_mix_pre, norm_mix_post, norm_ff_pre, norm_ff_post, ret_decay, attn_sink):
    batch, seq, d = x.shape
    depth = w_in.shape[0]
    ltok = seq + N_META
    assert d == D_MODEL and seq % (2 * CHUNK) == 0 and seq >= BAND and ltok % IN_TM == 0 and ltok % POST_TM == 0
    tables = _rotary_tables(seq)
    gains = [g.astype(F32).reshape(depth, 1, d) for g in (norm_mix_pre, norm_mix_post, norm_ff_pre, norm_ff_post)]
    post_weights = tuple(w.astype(F32) for w in (w_ret_o, w_att_o, w_mix_o, w_ff1, w_ff2))
    w_in_f = w_in.astype(F32)
    w_in_b = w_in_f[0].astype(BF16)
    for l in range(depth):
        last = l == depth - 1
        if l == 0:
            mix, gates, h = _in_proj(x.astype(F32), gains[0], w_in_b, tables, l, batch, ltok,
                                     meta=meta_tokens.astype(F32))
        else:
            mix, gates = _in_proj(h, gains[0], w_in_b, tables, l, batch, ltok)
        casts = ((post_weights, l),) + (() if last else (((w_in_f,), l + 1),))
        y_r, y_a, *copies = _mixers(mix, _decay_lanes(ret_decay[l]), attn_sink[l].astype(F32), batch, ltok, casts)
        post_b = copies[:len(post_weights)]
        if not last:
            w_in_b = copies[len(post_weights)]
        h = _post_ffn(h, y_r, y_a, gates, *post_b,
                      gains[1], gains[2], gains[3], l, batch, ltok, seq if last else ltok)
        if not last:
            h = h.reshape(batch * ltok, d)
    return h
```
